```python
import jax, jax.numpy as jnp
from jax import lax
import numpy as np

D_MODEL = 1024
BATCH = 4
SEQ = 8192
DEPTH = 2
DEC_BATCH = 32
DEC_SEQ = 8
PAST_LEN = 16384
PAGE_SIZE = 128

N_MEM = 256
N_AB_LAYERS = (DEPTH + 1) // 2
N_C_LAYERS = DEPTH // 2
A_WIDTH = D_MODEL // 2
A_HEAD = 64
A_HEADS = A_WIDTH // A_HEAD
DECAY_LORA = 64
AAA_LORA = 64
GATE_LORA = 128
A_PROJ = 3 * A_WIDTH + DECAY_LORA + AAA_LORA + GATE_LORA
A_SPLITS = (A_WIDTH, 2 * A_WIDTH, 3 * A_WIDTH, 3 * A_WIDTH + DECAY_LORA, 3 * A_WIDTH + DECAY_LORA + AAA_LORA)
B_WIDTH = D_MODEL - A_WIDTH
CONV_W = 3
AB_PROJ = A_PROJ + 3 * B_WIDTH
C_HEADS = 16
C_HEAD = D_MODEL // C_HEADS
C_WIDTH = C_HEADS * C_HEAD
C_SPLITS = (C_WIDTH, 2 * C_WIDTH, 3 * C_WIDTH, 3 * C_WIDTH + C_HEADS)
C_PROJ = 4 * C_WIDTH + C_HEADS
Q_BLOCK = 128
X_HEADS = 4
X_HEAD = D_MODEL // X_HEADS
D_FF = 2816
RMS_EPS = 1e-6
GN_EPS = 64e-5
L2_EPS = 1e-12

kernel_name = "rwkv7_shortconv_fox_memxattn_convffn_step"

F32 = jnp.float32


def rmsnorm(x, g):
    xf = x.astype(F32)
    y = xf * lax.rsqrt(jnp.mean(xf * xf, axis=-1, keepdims=True) + RMS_EPS)
    return (y * g.astype(F32)).astype(x.dtype)


def causal_dwconv(seq, w):
    t = seq.shape[1] - (CONV_W - 1)
    return sum(w[i] * seq[:, i:i + t] for i in range(CONV_W))


def wkv7_scan(r, decay, k, v, kk, a, s0):
    def step(s, inp):
        r_t, w_t, k_t, v_t, kk_t, a_t = inp
        s_kk = jnp.einsum('bhvk,bhk->bhv', s, kk_t)
        s = s * w_t[:, :, None, :] - s_kk[..., None] * (kk_t * a_t)[:, :, None, :] + v_t[..., :, None] * k_t[..., None, :]
        return s, jnp.einsum('bhvk,bhk->bhv', s, r_t)
    xs = tuple(jnp.moveaxis(t.astype(F32), 1, 0) for t in (r, decay, k, v, kk, a))
    s, ys = lax.scan(step, s0.astype(F32), xs)
    return jnp.moveaxis(ys, 0, 1), s


def head_groupnorm(o, w, b):
    mu = jnp.mean(o, axis=-1, keepdims=True)
    var = jnp.mean(jnp.square(o - mu), axis=-1, keepdims=True)
    y = (o - mu) * lax.rsqrt(var + GN_EPS)
    bsz, t = o.shape[:2]
    return y.reshape(bsz, t, A_WIDTH) * w.astype(F32) + b.astype(F32)


def ab_mixer(h, shift_prev, conv_prev, s0, w_in, w_out, mu, w0, w2, a0, a2, g2, k_k, k_a, r_k, ln_w, ln_b, conv_w):
    bsz, t, _ = h.shape
    proj = h @ w_in
    pa, pb = proj[..., :A_PROJ], proj[..., A_PROJ:]
    prev = jnp.concatenate([shift_prev[:, None].astype(pa.dtype), pa[:, :-1]], axis=1)
    xs = pa + (prev - pa) * mu
    r, k, v, wd, ad, gd = jnp.split(xs, A_SPLITS, axis=-1)
    w = -jax.nn.softplus(-(w0 + jnp.tanh(wd) @ w2)) - 0.5
    decay = jnp.exp(-jnp.exp(w.astype(F32)))
    a = jax.nn.sigmoid(a0 + ad @ a2)
    g = jax.nn.sigmoid(gd) @ g2
    heads = lambda z: z.reshape(bsz, t, A_HEADS, A_HEAD).astype(F32)
    kk = heads(k * k_k)
    kk = kk / jnp.maximum(jnp.sqrt(jnp.sum(kk * kk, axis=-1, keepdims=True)), L2_EPS)
    k = k * (1 + (a - 1) * k_a)
    rh, kh, vh, ah = heads(r), heads(k), heads(v), heads(a)
    o, s_new = wkv7_scan(rh, heads(decay), kh, vh, kk, ah, s0)
    bonus = jnp.sum(rh * kh * r_k.astype(F32), axis=-1, keepdims=True) * vh
    y_a = ((head_groupnorm(o, ln_w, ln_b) + bonus.reshape(bsz, t, A_WIDTH)) * g.astype(F32)).astype(h.dtype)
    gb, gc, hx = jnp.split(pb, 3, axis=-1)
    seq = jnp.concatenate([conv_prev.astype(pb.dtype), gc * hx], axis=1)
    y_b = gb * causal_dwconv(seq, conv_w)
    y = jnp.concatenate([y_a, y_b], axis=-1) @ w_out
    return y, pa[:, -1], seq[:, -(CONV_W - 1):], s_new


def fox_project(h, w_in, qg, kg, fb):
    bsz, t, _ = h.shape
    q, k, v, fl, og = jnp.split(h @ w_in, C_SPLITS, axis=-1)
    shp = (bsz, t, C_HEADS, C_HEAD)
    q = rmsnorm(q.reshape(shp), qg)
    k = rmsnorm(k.reshape(shp), kg)
    logf = jax.nn.log_sigmoid((fl + fb).astype(F32))
    return q, k, v.reshape(shp), logf, og


def fox_logits(q, cq, qpos, k, ck, kpos):
    s = jnp.einsum('bqhd,bkhd->bhqk', q, k).astype(F32) * (C_HEAD ** -0.5)
    bias = jnp.transpose(cq, (0, 2, 1))[..., :, None] - jnp.transpose(ck, (0, 2, 1))[..., None, :]
    causal = kpos[None, :] <= qpos[:, None]
    return jnp.where(causal, s + bias, -jnp.inf)


def fox_attend_prompt(q, k, v, logf):
    bsz, t = q.shape[:2]
    blk = Q_BLOCK if t % Q_BLOCK == 0 else t
    nb = t // blk
    cum = jnp.cumsum(logf, axis=1)
    pos = jnp.arange(t)
    qb = jnp.moveaxis(q.reshape(bsz, nb, blk, C_HEADS, C_HEAD), 1, 0)
    cb = jnp.moveaxis(cum.reshape(bsz, nb, blk, C_HEADS), 1, 0)
    pb = pos.reshape(nb, blk)

    def one_block(args):
        q_b, c_b, p_b = args
        p = jax.nn.softmax(fox_logits(q_b, c_b, p_b, k, cum, pos), axis=-1).astype(v.dtype)
        return jnp.einsum('bhqk,bkhd->bqhd', p, v)

    out = lax.map(one_block, (qb, cb, pb))
    return jnp.moveaxis(out, 0, 1).reshape(bsz, t, C_HEADS, C_HEAD)


def fox_attend_paged(q, k, v, logf, pool_k, pool_v, pool_logf, layer, page_table):
    bsz, t = q.shape[:2]
    past = page_table.shape[1] * PAGE_SIZE
    k_past = pool_k[layer, page_table].reshape(bsz, past, C_HEADS, C_HEAD)
    v_past = pool_v[layer, page_table].reshape(bsz, past, C_HEADS, C_HEAD)
    lf_past = pool_logf[layer, page_table].reshape(bsz, past, C_HEADS).astype(F32)
    cum = jnp.cumsum(jnp.concatenate([lf_past, logf], axis=1), axis=1)
    c_past, c_new = cum[:, :past], cum[:, past:]
    qpos = past + jnp.arange(t)
    s = jnp.concatenate([fox_logits(q, c_new, qpos, k_past, c_past, jnp.arange(past)),
                         fox_logits(q, c_new, qpos, k, c_new, qpos)], axis=-1)
    p = jax.nn.softmax(s, axis=-1).astype(v.dtype)
    return (jnp.einsum('bhqk,bkhd->bqhd', p[..., :past], v_past)
            + jnp.einsum('bhqk,bkhd->bqhd', p[..., past:], v))


def fox_output(o, og, w_out):
    bsz, t = o.shape[:2]
    return (o.reshape(bsz, t, C_WIDTH) * jax.nn.sigmoid(og)) @ w_out


def mem_kv(mem_h, w_kv, kg):
    bsz, m, _ = mem_h.shape
    kv = mem_h @ w_kv
    k = rmsnorm(kv[..., :D_MODEL].reshape(bsz, m, X_HEADS, X_HEAD), kg)
    v = kv[..., D_MODEL:].reshape(bsz, m, X_HEADS, X_HEAD)
    return k, v


def cross_attn(h, mk, mv, w_q, qg, w_o):
    bsz, t, _ = h.shape
    q = rmsnorm((h @ w_q).reshape(bsz, t, X_HEADS, X_HEAD), qg)
    s = jnp.einsum('bqhd,bmhd->bhqm', q, mk).astype(F32) * (X_HEAD ** -0.5)
    p = jax.nn.softmax(s, axis=-1).astype(mv.dtype)
    return jnp.einsum('bhqm,bmhd->bqhd', p, mv).reshape(bsz, t, D_MODEL) @ w_o


def conv_ffn(h, conv_prev, w_up, conv_w, w_down):
    u = h @ w_up
    gate, up = u[..., :D_FF], u[..., D_FF:]
    seq = jnp.concatenate([conv_prev.astype(gate.dtype), gate], axis=1)
    return (jax.nn.silu(causal_dwconv(seq, conv_w)) * up) @ w_down, seq[:, -(CONV_W - 1):]


def setup_inputs(seed: int = 0) -> dict:
    key = jax.random.key(seed)
    ks = iter(jax.random.split(key, 64))
    d = D_MODEL

    def nrm(shape, scale=1.0):
        return scale * jax.random.normal(next(ks), shape, F32)

    def gain(shape):
        return 1.0 + nrm(shape, 0.05)

    n_pages = PAST_LEN // PAGE_SIZE
    n_used = DEC_BATCH * n_pages
    n_pool = n_used + (n_used + 3) // 4
    page_table = jax.random.permutation(next(ks), n_pool)[:n_used].reshape(DEC_BATCH, n_pages).astype(jnp.int32)
    x_prompt = nrm((BATCH, SEQ, d))
    x_sample = nrm((DEC_BATCH, DEC_SEQ, d))
    mem_prompt = nrm((BATCH, N_MEM, d))
    state_wkv = nrm((N_AB_LAYERS, DEC_BATCH, A_HEADS, A_HEAD, A_HEAD), 0.3)
    state_shift = nrm((N_AB_LAYERS, DEC_BATCH, A_PROJ))
    state_conv = nrm((N_AB_LAYERS, DEC_BATCH, CONV_W - 1, B_WIDTH))
    state_ffn_conv = nrm((DEPTH, DEC_BATCH, CONV_W - 1, D_FF))
    cache_k = nrm((N_C_LAYERS, n_pool, PAGE_SIZE, C_HEADS, C_HEAD))
    cache_v = nrm((N_C_LAYERS, n_pool, PAGE_SIZE, C_HEADS, C_HEAD))
    cache_logf = jax.nn.log_sigmoid(4.0 + nrm((N_C_LAYERS, n_pool, PAGE_SIZE, C_HEADS)))
    cache_mem_k = nrm((DEPTH, DEC_BATCH, N_MEM, X_HEADS, X_HEAD))
    cache_mem_v = nrm((DEPTH, DEC_BATCH, N_MEM, X_HEADS, X_HEAD))
    return {
        "x_prompt": x_prompt, "x_sample": x_sample, "mem_prompt": mem_prompt,
        "state_wkv": state_wkv, "state_shift": state_shift, "state_conv": state_conv,
        "state_ffn_conv": state_ffn_conv, "cache_k": cache_k, "cache_v": cache_v,
        "cache_logf": cache_logf, "cache_mem_k": cache_mem_k, "cache_mem_v": cache_mem_v,
        "page_table": page_table,
        "norm_mix": gain((DEPTH, d)), "norm_cross": gain((DEPTH, d)),
        "norm_mem": gain((DEPTH, d)), "norm_ffn": gain((DEPTH, d)),
        "ab_in": nrm((N_AB_LAYERS, d, AB_PROJ), d ** -0.5),
        "ab_out": nrm((N_AB_LAYERS, A_WIDTH + B_WIDTH, d), (A_WIDTH + B_WIDTH) ** -0.5),
        "rw_mu": jax.random.uniform(next(ks), (N_AB_LAYERS, A_PROJ), F32),
        "rw_w0": -1.0 + nrm((N_AB_LAYERS, A_WIDTH), 0.5),
        "rw_w2": nrm((N_AB_LAYERS, DECAY_LORA, A_WIDTH), DECAY_LORA ** -0.5),
        "rw_a0": nrm((N_AB_LAYERS, A_WIDTH), 0.5),
        "rw_a2": nrm((N_AB_LAYERS, AAA_LORA, A_WIDTH), AAA_LORA ** -0.5),
        "rw_g2": nrm((N_AB_LAYERS, GATE_LORA, A_WIDTH), GATE_LORA ** -0.5),
        "rw_kk": 0.85 + nrm((N_AB_LAYERS, A_WIDTH), 0.05),
        "rw_ka": gain((N_AB_LAYERS, A_WIDTH)),
        "rw_rk": nrm((N_AB_LAYERS, A_HEADS, A_HEAD), 0.1),
        "rw_lnw": gain((N_AB_LAYERS, A_WIDTH)),
        "rw_lnb": nrm((N_AB_LAYERS, A_WIDTH), 0.01),
        "sc_conv_w": nrm((N_AB_LAYERS, CONV_W, B_WIDTH), 0.5),
        "fox_in": nrm((N_C_LAYERS, d, C_PROJ), d ** -0.5),
        "fox_out": nrm((N_C_LAYERS, C_WIDTH, d), C_WIDTH ** -0.5),
        "fox_qg": gain((N_C_LAYERS, C_HEAD)), "fox_kg": gain((N_C_LAYERS, C_HEAD)),
        "fox_fb": 4.0 + nrm((N_C_LAYERS, C_HEADS), 0.5),
        "xq_w": nrm((DEPTH, d, d), d ** -0.5), "xkv_w": nrm((DEPTH, d, 2 * d), d ** -0.5),
        "xo_w": nrm((DEPTH, d, d), d ** -0.5),
        "xq_g": gain((DEPTH, X_HEAD)), "xk_g": gain((DEPTH, X_HEAD)),
        "ffn_up": nrm((DEPTH, d, 2 * D_FF), d ** -0.5),
        "ffn_conv_w": nrm((DEPTH, CONV_W, D_FF), 0.5),
        "ffn_down": nrm((DEPTH, D_FF, d), D_FF ** -0.5),
    }


def reference(x_prompt, x_sample, mem_prompt, state_wkv, state_shift, state_conv, state_ffn_conv,
              cache_k, cache_v, cache_logf, cache_mem_k, cache_mem_v, page_table,
              norm_mix, norm_cross, norm_mem, norm_ffn, ab_in, ab_out, rw_mu, rw_w0, rw_w2, rw_a0, rw_a2,
              rw_g2, rw_kk, rw_ka, rw_rk, rw_lnw, rw_lnb, sc_conv_w, fox_in, fox_out, fox_qg, fox_kg, fox_fb,
              xq_w, xkv_w, xo_w, xq_g, xk_g, ffn_up, ffn_conv_w, ffn_down):
    bp = x_prompt.shape[0]
    yp, ys = x_prompt, x_sample
    pw, psh, pcv, pfc, pk, pv, plf, pmk, pmv = [], [], [], [], [], [], [], [], []
    sw, ssh, scv, sfc, sk, sv, slf = [], [], [], [], [], [], []
    for layer in range(DEPTH):
        i = layer // 2
        hp = rmsnorm(yp, norm_mix[layer])
        hs = rmsnorm(ys, norm_mix[layer])
        if layer % 2 == 0:
            prm = (ab_in[i], ab_out[i], rw_mu[i], rw_w0[i], rw_w2[i], rw_a0[i], rw_a2[i], rw_g2[i],
                   rw_kk[i], rw_ka[i], rw_rk[i], rw_lnw[i], rw_lnb[i], sc_conv_w[i])
            op, sh, cv, st = ab_mixer(hp, jnp.zeros((bp, A_PROJ), hp.dtype),
                                      jnp.zeros((bp, CONV_W - 1, B_WIDTH), hp.dtype),
                                      jnp.zeros((bp, A_HEADS, A_HEAD, A_HEAD), F32), *prm)
            pw.append(st); psh.append(sh); pcv.append(cv)
            os_, sh, cv, st = ab_mixer(hs, state_shift[i], state_conv[i], state_wkv[i], *prm)
            sw.append(st); ssh.append(sh); scv.append(cv)
        else:
            q, k, v, lf, og = fox_project(hp, fox_in[i], fox_qg[i], fox_kg[i], fox_fb[i])
            op = fox_output(fox_attend_prompt(q, k, v, lf), og, fox_out[i])
            pk.append(k); pv.append(v); plf.append(lf)
            q, k, v, lf, og = fox_project(hs, fox_in[i], fox_qg[i], fox_kg[i], fox_fb[i])
            os_ = fox_output(fox_attend_paged(q, k, v, lf, cache_k, cache_v, cache_logf, i, page_table), og, fox_out[i])
            sk.append(k); sv.append(v); slf.append(lf)
        yp = yp + op
        ys = ys + os_
        mk, mv = mem_kv(rmsnorm(mem_prompt, norm_mem[layer]), xkv_w[layer], xk_g[layer])
        pmk.append(mk); pmv.append(mv)
        yp = yp + cross_attn(rmsnorm(yp, norm_cross[layer]), mk, mv, xq_w[layer], xq_g[layer], xo_w[layer])
        ys = ys + cross_attn(rmsnorm(ys, norm_cross[layer]), cache_mem_k[layer], cache_mem_v[layer],
                             xq_w[layer], xq_g[layer], xo_w[layer])
        fp, cvp = conv_ffn(rmsnorm(yp, norm_ffn[layer]), jnp.zeros((bp, CONV_W - 1, D_FF), yp.dtype),
                           ffn_up[layer], ffn_conv_w[layer], ffn_down[layer])
        fs, cvs = conv_ffn(rmsnorm(ys, norm_ffn[layer]), state_ffn_conv[layer],
                           ffn_up[layer], ffn_conv_w[layer], ffn_down[layer])
        pfc.append(cvp); sfc.append(cvs)
        yp = yp + fp
        ys = ys + fs
    p_wkv, p_shift, p_conv, p_ffn_conv = jnp.stack(pw), jnp.stack(psh), jnp.stack(pcv), jnp.stack(pfc)
    p_k, p_v, p_logf = jnp.stack(pk), jnp.stack(pv), jnp.stack(plf)
    p_mem_k, p_mem_v = jnp.stack(pmk), jnp.stack(pmv)
    s_wkv, s_shift, s_conv, s_ffn_conv = jnp.stack(sw), jnp.stack(ssh), jnp.stack(scv), jnp.stack(sfc)
    s_k, s_v, s_logf = jnp.stack(sk), jnp.stack(sv), jnp.stack(slf)
    return (yp, ys, p_wkv, p_shift, p_conv, p_ffn_conv, p_k, p_v, p_logf, p_mem_k, p_mem_v,
            s_wkv, s_shift, s_conv, s_ffn_conv, s_k, s_v, s_logf)
```

```python
import functools

import jax
import jax.numpy as jnp
from jax import lax
from jax.experimental import pallas as pl
from jax.experimental.pallas import tpu as pltpu

F32 = jnp.float32
BF16 = jnp.bfloat16

RMS_EPS = 1e-6
GN_EPS = 64e-5
L2_EPS = 1e-12
CONV_W = 3
A_HEAD = 64
C_HEAD = 64
X_HEADS = 4
PAGE = 128
LANES = 128
SUBLANES = 8
VMEM_LIMIT = 56 * 1024 * 1024
NEG = -1e30


def _bdot(a, b):
    return jnp.dot(a.astype(BF16), b.astype(BF16), preferred_element_type=F32)


def _bdot_nt(a, b):
    return lax.dot_general(a.astype(BF16), b.astype(BF16), (((1,), (1,)), ((), ())), preferred_element_type=F32)


def _split3(x):
    x1 = x.astype(BF16)
    r1 = x - x1.astype(F32)
    x2 = r1.astype(BF16)
    x3 = (r1 - x2.astype(F32)).astype(BF16)
    return x1, x2, x3


def _seg_sum(x, bb):
    w = bb.shape[0]
    outs = []
    for c in range(x.shape[1] // w):
        xc = x[:, c * w:(c + 1) * w]
        hi = xc.astype(BF16)
        lo = (xc - hi.astype(F32)).astype(BF16)
        outs.append(jnp.dot(hi, bb, preferred_element_type=F32) + jnp.dot(lo, bb, preferred_element_type=F32))
    return outs[0] if len(outs) == 1 else jnp.concatenate(outs, axis=1)


def _seg_sum1(xb, bb):
    w = bb.shape[0]
    outs = [jnp.dot(xb[:, c * w:(c + 1) * w], bb, preferred_element_type=F32) for c in range(xb.shape[1] // w)]
    return outs[0] if len(outs) == 1 else jnp.concatenate(outs, axis=1)


def _rms_rows(x, g):
    return x * lax.rsqrt(jnp.mean(x * x, axis=-1, keepdims=True) + RMS_EPS) * g


def _sigmoid(x):
    return 1.0 / (1.0 + jnp.exp(-x))


def _softplus(x):
    return jnp.maximum(x, 0.0) + jnp.log(1.0 + jnp.exp(-jnp.abs(x)))


def _pos_in_seq(rows, seg_len):
    if seg_len & (seg_len - 1) == 0:
        return jnp.bitwise_and(rows, seg_len - 1)
    return lax.rem(rows, seg_len)


def _idiv(x, n):
    if n & (n - 1) == 0:
        return jnp.right_shift(x, n.bit_length() - 1)
    return lax.div(x, n)


def _taps(x, rows, seg_len, tail, e0, e1, want2=True):
    r1 = pltpu.roll(x, 1, 0)
    r2 = pltpu.roll(x, 2, 0) if want2 else None
    if tail is not None:
        c1 = tail[SUBLANES - 1:SUBLANES, :]
        xm1 = jnp.where(rows == 0, c1, r1)
        xm2 = None
        if want2:
            c0 = tail[SUBLANES - 2:SUBLANES - 1, :]
            xm2 = jnp.where(rows == 0, c0, jnp.where(rows == 1, c1, r2))
        return xm1, xm2
    pos = _pos_in_seq(rows, seg_len)
    xm1 = jnp.where(pos == 0, e1, r1)
    xm2 = jnp.where(pos == 0, e0, jnp.where(pos == 1, e1, r2)) if want2 else None
    return xm1, xm2


def _cparams(sem, vmem=VMEM_LIMIT):
    return pltpu.CompilerParams(dimension_semantics=sem, vmem_limit_bytes=vmem)


def _const_spec(shape):
    nd = len(shape)
    return pl.BlockSpec(shape, lambda *_: (0,) * nd, pipeline_mode=pl.Buffered(1))


def _ab_in_kernel(carry_mode, seg_len, tiles_per_seq, a_width, b_width, *refs):
    (x_ref, gm_ref, win_ref, mu_ref, w0_ref, w2_ref, a0_ref, a2_ref, g2_ref, kkw_ref, ka_ref, rk_ref,
     cw_ref, bb_ref) = refs[:14]
    if carry_mode:
        init_pa_ref, init_sq_ref = refs[14:16]
        outs = refs[16:27]
        pa_tail, sq_tail = refs[27:29]
    else:
        e_pa_ref, e_sq0_ref, e_sq1_ref = refs[14:17]
        outs = refs[17:28]
    (r_o, w_o, k_o, v_o, kk_o, b_o, bonus_o, g_o, yb_o, pa_o, sq_o) = outs
    tm = x_ref.shape[0]
    a_proj = 3 * a_width + 2 * LANES
    rows = lax.broadcasted_iota(jnp.int32, (tm, 1), 0)

    if carry_mode:
        @pl.when(pl.program_id(0) % tiles_per_seq == 0)
        def _():
            pa_tail[...] = init_pa_ref[0]
            sq_tail[...] = init_sq_ref[0]

    x = x_ref[...]
    h = _rms_rows(x, gm_ref[...]).astype(BF16)
    proj = jnp.dot(h, win_ref[...], preferred_element_type=F32)
    pa = proj[:, :a_proj]
    pb = proj[:, a_proj:]

    if carry_mode:
        prev, _ = _taps(pa, rows, seg_len, pa_tail[...], None, None, want2=False)
    else:
        prev, _ = _taps(pa, rows, seg_len, None, None, e_pa_ref[...], want2=False)
    xs = pa + (prev - pa) * mu_ref[...]
    r = xs[:, :a_width]
    k = xs[:, a_width:2 * a_width]
    v = xs[:, 2 * a_width:3 * a_width]
    wa = xs[:, 3 * a_width:3 * a_width + LANES]
    gd = xs[:, 3 * a_width + LANES:a_proj]
    wl = w0_ref[...] + _bdot(jnp.tanh(wa), w2_ref[...])
    w = -_softplus(-wl) - 0.5
    decay = jnp.exp(-jnp.exp(w))
    a = _sigmoid(a0_ref[...] + _bdot(wa, a2_ref[...]))
    g = _bdot(_sigmoid(gd), g2_ref[...])
    bb = bb_ref[...]
    kk = k * kkw_ref[...]
    kk = kk / jnp.maximum(jnp.sqrt(_seg_sum(kk * kk, bb)), L2_EPS)
    k2 = k * (1.0 + (a - 1.0) * ka_ref[...])
    bonus = _seg_sum(r * k2 * rk_ref[...], bb) * v
    r_o[...] = r
    w_o[...] = decay
    k_o[...] = k2
    v_o[...] = v
    kk_o[...] = kk
    b_o[...] = kk * a
    bonus_o[...] = bonus
    g_o[...] = g

    gb = pb[:, :b_width]
    sq = pb[:, b_width:2 * b_width] * pb[:, 2 * b_width:]
    if carry_mode:
        xm1, xm2 = _taps(sq, rows, seg_len, sq_tail[...], None, None)
    else:
        xm1, xm2 = _taps(sq, rows, seg_len, None, e_sq0_ref[...], e_sq1_ref[...])
    cw = cw_ref[...]
    yb_o[...] = gb * (cw[0:1, :] * xm2 + cw[1:2, :] * xm1 + cw[2:3, :] * sq)

    if carry_mode:
        pa_last = pa[tm - SUBLANES:, :]
        sq_last = sq[tm - SUBLANES:, :]
        pa_tail[...] = pa_last
        sq_tail[...] = sq_last
        pa_o[0] = pa_last
        sq_o[0] = sq_last
    else:
        pa_o[...] = pa
        sq_o[...] = sq


def _tail8(state):
    b, n, c = state.shape
    return jnp.concatenate([jnp.zeros((b, SUBLANES - n, c), state.dtype), state], axis=1)


def _ab_in(x, seg_len, tm, shift_prev, conv_prev, prm):
    n, d = x.shape
    n_seq = n // seg_len
    a_width = prm["w0"].shape[1]
    b_width = prm["cw"].shape[1]
    a_proj = 3 * a_width + 2 * LANES
    carry_mode = seg_len >= tm
    tiles_per_seq = max(seg_len // tm, 1)
    grid = (n // tm,)
    row = lambda c: pl.BlockSpec((tm, c), lambda i: (i, 0))
    params = [prm["gm"], prm["win"], prm["mu"], prm["w0"], prm["w2p"], prm["a0"], prm["a2p"], prm["g2"],
              prm["kkw"], prm["ka"], prm["rk"], prm["cw"], prm["bb"]]
    in_specs = [row(d)] + [_const_spec(p.shape) for p in params]
    out_shape = [jax.ShapeDtypeStruct((n, a_width), F32)] * 8 + [jax.ShapeDtypeStruct((n, b_width), F32)]
    out_specs = [row(a_width)] * 8 + [row(b_width)]
    if carry_mode:
        seq_blk = lambda c: pl.BlockSpec((1, SUBLANES, c), lambda i: (i // tiles_per_seq, 0, 0))
        extra = [_tail8(shift_prev[:, None, :]), _tail8(conv_prev)]
        in_specs += [seq_blk(a_proj), seq_blk(b_width)]
        out_shape += [jax.ShapeDtypeStruct((n_seq, SUBLANES, a_proj), F32),
                      jax.ShapeDtypeStruct((n_seq, SUBLANES, b_width), F32)]
        out_specs += [seq_blk(a_proj), seq_blk(b_width)]
        scratch = [pltpu.VMEM((SUBLANES, a_proj), F32), pltpu.VMEM((SUBLANES, b_width), F32)]
    else:
        extra = [jnp.repeat(shift_prev, seg_len, axis=0), jnp.repeat(conv_prev[:, 0], seg_len, axis=0),
                 jnp.repeat(conv_prev[:, 1], seg_len, axis=0)]
        in_specs += [row(a_proj), row(b_width), row(b_width)]
        out_shape += [jax.ShapeDtypeStruct((n, a_proj), F32), jax.ShapeDtypeStruct((n, b_width), F32)]
        out_specs += [row(a_proj), row(b_width)]
        scratch = []
    outs = pl.pallas_call(
        functools.partial(_ab_in_kernel, carry_mode, seg_len, tiles_per_seq, a_width, b_width),
        grid=grid, in_specs=in_specs, out_specs=out_specs, out_shape=out_shape, scratch_shapes=scratch,
        compiler_params=_cparams(("arbitrary",)),
    )(x, *params, *extra)
    scan_ops = outs[:6]
    bonus, g, yb, pa_t, sq_t = outs[6:]
    if carry_mode:
        shift_new = pa_t[:, SUBLANES - 1]
        conv_new = sq_t[:, SUBLANES - (CONV_W - 1):]
    else:
        shift_new = pa_t.reshape(n_seq, seg_len, a_proj)[:, -1]
        conv_new = sq_t.reshape(n_seq, seg_len, b_width)[:, -(CONV_W - 1):]
    return scan_ops, bonus, g, yb, shift_new, conv_new


def _wkv_kernel(r_ref, w_ref, k_ref, v_ref, kk_ref, b_ref, s0_ref, bb_ref, eye_ref, o_ref, s_out_ref, s_scr):
    nb, tc, _ = r_ref.shape

    @pl.when(pl.program_id(1) == 0)
    def _():
        s_scr[...] = s0_ref[...]

    bb = bb_ref[...]
    eye = eye_ref[...]

    def step(t, carry):
        for i in range(nb):
            row = lambda ref: ref[i, pl.ds(t, 1), :]
            s = s_scr[i]
            sk = _seg_sum1((s * row(kk_ref)).astype(BF16), bb)
            vb = _seg_sum1((eye * row(v_ref)).astype(BF16), bb)
            s = s * row(w_ref) - sk * row(b_ref) + vb * row(k_ref)
            s_scr[i] = s
            yb = _seg_sum1((s * row(r_ref)).astype(BF16), bb)
            o_ref[i, pl.ds(t, 1), :] = jnp.sum(yb * eye, axis=0, keepdims=True)
        return carry

    lax.fori_loop(0, tc, step, 0)

    @pl.when(pl.program_id(1) == pl.num_programs(1) - 1)
    def _():
        s_out_ref[...] = s_scr[...]


def _wkv(scan_ops, n_seq, seg_len, s0, bb, eye, nb, tc):
    width = scan_ops[0].shape[1]
    heads = width // A_HEAD
    ops3 = [a.reshape(n_seq, seg_len, width) for a in scan_ops]
    s0_l = jnp.transpose(s0.astype(F32), (0, 2, 1, 3)).reshape(n_seq, A_HEAD, width)
    seq_blk = pl.BlockSpec((nb, tc, width), lambda b, c: (b, c, 0))
    st_blk = pl.BlockSpec((nb, A_HEAD, width), lambda b, c: (b, 0, 0))
    o, s_new = pl.pallas_call(
        _wkv_kernel, grid=(n_seq // nb, seg_len // tc),
        in_specs=[seq_blk] * 6 + [st_blk, _const_spec(bb.shape), _const_spec(eye.shape)],
        out_specs=[seq_blk, st_blk],
        out_shape=[jax.ShapeDtypeStruct((n_seq, seg_len, width), F32),
                   jax.ShapeDtypeStruct((n_seq, A_HEAD, width), F32)],
        scratch_shapes=[pltpu.VMEM((nb, A_HEAD, width), F32)],
        compiler_params=_cparams(("arbitrary", "arbitrary")),
    )(*ops3, s0_l, bb, eye)
    s_new = jnp.transpose(s_new.reshape(n_seq, A_HEAD, heads, A_HEAD), (0, 2, 1, 3))
    return o.reshape(n_seq * seg_len, width), s_new


def _ab_out_kernel(o_ref, bonus_ref, g_ref, yb_ref, x_ref, lnw_ref, lnb_ref, wout_ref, bb_ref, y_ref):
    a_width = o_ref.shape[1]
    o = o_ref[...]
    bb = bb_ref[...]
    mean = _seg_sum(o, bb) * (1.0 / A_HEAD)
    dlt = o - mean
    var = _seg_sum(dlt * dlt, bb) * (1.0 / A_HEAD)
    yn = dlt * lax.rsqrt(var + GN_EPS) * lnw_ref[...] + lnb_ref[...]
    ya = (yn + bonus_ref[...]) * g_ref[...]
    y = _bdot(ya, wout_ref[:a_width, :]) + _bdot(yb_ref[...], wout_ref[a_width:, :])
    y_ref[...] = x_ref[...] + y


def _ab_out(o, bonus, g, yb, x, prm, tm):
    n, d = x.shape
    row = lambda c: pl.BlockSpec((tm, c), lambda i: (i, 0))
    params = [prm["lnw"], prm["lnb"], prm["wout"], prm["bb"]]
    return pl.pallas_call(
        _ab_out_kernel, grid=(n // tm,),
        in_specs=[row(o.shape[1]), row(bonus.shape[1]), row(g.shape[1]), row(yb.shape[1]), row(d)]
        + [_const_spec(p.shape) for p in params],
        out_specs=row(d), out_shape=jax.ShapeDtypeStruct((n, d), F32),
        compiler_params=_cparams(("arbitrary",)),
    )(o, bonus, g, yb, x, *params)


def _mem_kv_kernel(x_ref, gm_ref, wkv_ref, kg_ref, k_ref, v_ref):
    d = x_ref.shape[1]
    hd = d // X_HEADS
    h = _rms_rows(x_ref[...], gm_ref[...])
    kv = _bdot(h, wkv_ref[...])
    kg = kg_ref[...]
    for hh in range(X_HEADS):
        kh = kv[:, hh * hd:(hh + 1) * hd]
        k_ref[:, hh * hd:(hh + 1) * hd] = _rms_rows(kh, kg)
    v_ref[...] = kv[:, d:]


def _mem_kv(mem, gm, wkv, kg, tm):
    n, d = mem.shape
    row = pl.BlockSpec((tm, d), lambda i: (i, 0))
    return pl.pallas_call(
        _mem_kv_kernel, grid=(n // tm,),
        in_specs=[row, _const_spec(gm.shape), _const_spec(wkv.shape), _const_spec(kg.shape)],
        out_specs=[row, row], out_shape=[jax.ShapeDtypeStruct((n, d), F32)] * 2,
        compiler_params=_cparams(("arbitrary",)),
    )(mem, gm, wkv, kg)


def _cross_kernel(x_ref, mk_ref, mv_ref, gm_ref, wq_ref, qg_ref, wo_ref, y_ref, o_scr):
    d = x_ref.shape[1]
    hd = d // X_HEADS
    x = x_ref[...]
    q = _bdot(_rms_rows(x, gm_ref[...]), wq_ref[...])
    qg = qg_ref[...]
    scale = hd ** -0.5
    for hh in range(X_HEADS):
        sl = slice(hh * hd, (hh + 1) * hd)
        qh = _rms_rows(q[:, sl], qg)
        s = _bdot_nt(qh, mk_ref[0, :, sl]) * scale
        p = jnp.exp(s - jnp.max(s, axis=-1, keepdims=True))
        p = p / jnp.sum(p, axis=-1, keepdims=True)
        o_scr[:, sl] = _bdot(p, mv_ref[0, :, sl])
    y_ref[...] = x + _bdot(o_scr[...], wo_ref[...])


def _cross(x, seg_len, tm, mk, mv, gm, wq, qg, wo):
    n, d = x.shape
    tiles_per_seq = seg_len // tm
    row = pl.BlockSpec((tm, d), lambda i: (i, 0))
    mem = pl.BlockSpec((1, mk.shape[1], d), lambda i: (i // tiles_per_seq, 0, 0))
    params = [gm, wq, qg, wo]
    return pl.pallas_call(
        _cross_kernel, grid=(n // tm,),
        in_specs=[row, mem, mem] + [_const_spec(p.shape) for p in params],
        out_specs=row, out_shape=jax.ShapeDtypeStruct((n, d), F32),
        scratch_shapes=[pltpu.VMEM((tm, d), F32)],
        compiler_params=_cparams(("arbitrary",)),
    )(x, mk, mv, *params)


def _ffn_kernel(carry_mode, seg_len, tiles_per_seq, n_chunks, *refs):
    x_ref, gm_ref, wup_ref, cw_ref, wdn_ref = refs[:5]
    if carry_mode:
        init_ref, y_ref, gate_o, tail = refs[5:9]
    else:
        e0_ref, e1_ref, y_ref, gate_o = refs[5:9]
    tm = x_ref.shape[0]
    d_ff = cw_ref.shape[1]
    ck = d_ff // n_chunks
    rows = lax.broadcasted_iota(jnp.int32, (tm, 1), 0)

    if carry_mode:
        @pl.when(pl.program_id(0) % tiles_per_seq == 0)
        def _():
            tail[...] = init_ref[0]

    x = x_ref[...]
    h = _rms_rows(x, gm_ref[...]).astype(BF16)
    acc = x
    for c in range(n_chunks):
        sl = slice(c * ck, (c + 1) * ck)
        gate = jnp.dot(h, wup_ref[:, sl], preferred_element_type=F32)
        up = jnp.dot(h, wup_ref[:, d_ff + c * ck:d_ff + (c + 1) * ck], preferred_element_type=F32)
        if carry_mode:
            xm1, xm2 = _taps(gate, rows, seg_len, tail[:, sl], None, None)
        else:
            xm1, xm2 = _taps(gate, rows, seg_len, None, e0_ref[:, sl], e1_ref[:, sl])
        cv = cw_ref[0:1, sl] * xm2 + cw_ref[1:2, sl] * xm1 + cw_ref[2:3, sl] * gate
        act = cv * _sigmoid(cv) * up
        acc = acc + _bdot(act, wdn_ref[sl, :])
        if carry_mode:
            last = gate[tm - SUBLANES:, :]
            tail[:, sl] = last
            gate_o[0, :, sl] = last
        else:
            gate_o[:, sl] = gate
    y_ref[...] = acc


def _ffn(x, seg_len, tm, conv_prev, gm, wup, cw, wdn, n_chunks=2):
    n, d = x.shape
    d_ff = cw.shape[1]
    n_seq = n // seg_len
    carry_mode = seg_len >= tm
    tiles_per_seq = max(seg_len // tm, 1)
    row = lambda c: pl.BlockSpec((tm, c), lambda i: (i, 0))
    params = [gm, wup, cw, wdn]
    in_specs = [row(d)] + [_const_spec(p.shape) for p in params]
    if carry_mode:
        seq_blk = pl.BlockSpec((1, SUBLANES, d_ff), lambda i: (i // tiles_per_seq, 0, 0))
        extra = [_tail8(conv_prev)]
        in_specs += [seq_blk]
        out_specs = [row(d), seq_blk]
        out_shape = [jax.ShapeDtypeStruct((n, d), F32), jax.ShapeDtypeStruct((n_seq, SUBLANES, d_ff), F32)]
        scratch = [pltpu.VMEM((SUBLANES, d_ff), F32)]
    else:
        extra = [jnp.repeat(conv_prev[:, 0], seg_len, axis=0), jnp.repeat(conv_prev[:, 1], seg_len, axis=0)]
        in_specs += [row(d_ff), row(d_ff)]
        out_specs = [row(d), row(d_ff)]
        out_shape = [jax.ShapeDtypeStruct((n, d), F32), jax.ShapeDtypeStruct((n, d_ff), F32)]
        scratch = []
    y, gate_t = pl.pallas_call(
        functools.partial(_ffn_kernel, carry_mode, seg_len, tiles_per_seq, n_chunks),
        grid=(n // tm,), in_specs=in_specs, out_specs=out_specs, out_shape=out_shape, scratch_shapes=scratch,
        compiler_params=_cparams(("arbitrary",)),
    )(x, *params, *extra)
    if carry_mode:
        conv_new = gate_t[:, SUBLANES - (CONV_W - 1):]
    else:
        conv_new = gate_t.reshape(n_seq, seg_len, d_ff)[:, -(CONV_W - 1):]
    return y, conv_new


def _log_sigmoid(x):
    return jnp.minimum(x, 0.0) - jnp.log(1.0 + jnp.exp(-jnp.abs(x)))


def _fox_in_kernel(with_cum, tiles_per_seq, *refs):
    x_ref, gm_ref, w_ref, wfl_ref, fb_ref, qg_ref, kg_ref, bb_ref = refs[:8]
    if with_cum:
        wflt_ref, fbt_ref, tri_ref = refs[8:11]
        q_o, k_o, v_o, og_o, lf_o, cum_o, run = refs[11:18]
    else:
        q_o, k_o, v_o, og_o, lf_o = refs[8:13]
    d = x_ref.shape[1]
    heads = lf_o.shape[1]
    h = _rms_rows(x_ref[...], gm_ref[...]).astype(BF16)
    proj = jnp.dot(h, w_ref[...], preferred_element_type=F32)
    bb = bb_ref[...]
    inv = 1.0 / C_HEAD

    def head_norm(z, gain):
        return z * lax.rsqrt(_seg_sum(z * z, bb) * inv + RMS_EPS) * gain

    q_o[...] = (head_norm(proj[:, :d], qg_ref[...]) * (C_HEAD ** -0.5)).astype(q_o.dtype)
    k_o[...] = head_norm(proj[:, d:2 * d], kg_ref[...])
    v_o[...] = proj[:, 2 * d:3 * d]
    og_o[...] = proj[:, 3 * d:]
    fl = jnp.dot(h, wfl_ref[...], preferred_element_type=F32) + fb_ref[...]
    lf_o[...] = _log_sigmoid(fl)[:, :heads]

    if with_cum:
        @pl.when(pl.program_id(0) % tiles_per_seq == 0)
        def _():
            run[...] = jnp.zeros_like(run)

        flt = lax.dot_general(wflt_ref[...], h, (((1,), (1,)), ((), ())), preferred_element_type=F32)
        lft = _log_sigmoid(flt + fbt_ref[...])
        tri = tri_ref[...]
        cum = run[...][:, :1]
        for part in _split3(lft):
            cum = cum + jnp.dot(part, tri, preferred_element_type=F32)
        cum_o[0] = cum
        run[...] = jnp.broadcast_to(cum[:, cum.shape[1] - 1:], run.shape)


def _fox_in(x, seg_len, tm, prm, with_cum, q_dtype):
    n, d = x.shape
    n_seq = n // seg_len
    heads = d // C_HEAD
    tiles_per_seq = max(seg_len // tm, 1)
    row = lambda c: pl.BlockSpec((tm, c), lambda i: (i, 0))
    params = [prm["gm"], prm["w"], prm["wfl"], prm["fb"], prm["qg"], prm["kg"], prm["bb"]]
    out_shape = [jax.ShapeDtypeStruct((n, d), q_dtype)] + [jax.ShapeDtypeStruct((n, d), F32)] * 3 + \
                [jax.ShapeDtypeStruct((n, heads), F32)]
    out_specs = [row(d)] * 4 + [row(heads)]
    scratch = []
    if with_cum:
        params += [prm["wflt"], prm["fbt"], prm["tri"]]
        out_shape += [jax.ShapeDtypeStruct((n_seq, heads, seg_len), F32)]
        out_specs += [pl.BlockSpec((1, heads, tm), lambda i: (i // tiles_per_seq, 0, i % tiles_per_seq))]
        scratch = [pltpu.VMEM((heads, LANES), F32)]
    return pl.pallas_call(
        functools.partial(_fox_in_kernel, with_cum, tiles_per_seq),
        grid=(n // tm,), in_specs=[row(d)] + [_const_spec(p.shape) for p in params],
        out_specs=out_specs, out_shape=out_shape, scratch_shapes=scratch,
        compiler_params=_cparams(("arbitrary",)),
    )(x, *params)


def _fox_attn_kernel(tk, q_ref, k_ref, v_ref, cum_ref, o_ref, acc, mrun):
    tq = q_ref.shape[0]
    pair = pl.program_id(1)
    qi = pl.program_id(2)
    lane = lax.broadcasted_iota(jnp.int32, (1, LANES), 1)
    first = lane < C_HEAD
    q = q_ref[...]
    zero = jnp.zeros_like(q)
    qs = (jnp.where(first, q, zero), jnp.where(first, zero, q))
    acc[...] = jnp.zeros_like(acc)
    mrun[...] = jnp.full_like(mrun, NEG)

    def block(j, masked):
        kb = k_ref[0, pl.ds(j * tk, tk), :].astype(BF16)
        vb = v_ref[0, pl.ds(j * tk, tk), :].astype(BF16)
        one = jnp.ones_like(vb)
        vs = (jnp.where(first, vb, one), jnp.where(first, one, vb))
        for hh in range(2):
            bias = -cum_ref[0, 2 * pair + hh, pl.ds(j, 1), :]
            s = _bdot_nt(qs[hh], kb) + bias
            if masked:
                rr = lax.broadcasted_iota(jnp.int32, (tq, tk), 0)
                cc = lax.broadcasted_iota(jnp.int32, (tq, tk), 1)
                s = jnp.where(cc <= rr, s, NEG)
            m_old = mrun[hh]
            m_new = jnp.maximum(m_old, jnp.max(s, axis=-1, keepdims=True))
            p = jnp.exp(s - m_new[:, :1])
            acc[hh] = jnp.exp(m_old - m_new) * acc[hh] + jnp.dot(p.astype(BF16), vs[hh],
                                                                 preferred_element_type=F32)
            mrun[hh] = m_new

    def body(j, carry):
        block(j, False)
        return carry

    lax.fori_loop(0, qi, body, 0)
    block(qi, True)
    a0 = acc[0]
    a1 = acc[1]
    o_ref[...] = jnp.where(first, a0 / pltpu.roll(a0, C_HEAD, 1), a1 / pltpu.roll(a1, C_HEAD, 1))


def _fox_attn(q, k, v, cum_t, n_seq, seg_len, tq):
    n, d = q.shape
    heads = d // C_HEAD
    nq = seg_len // tq
    k3 = k.reshape(n_seq, seg_len, d)
    v3 = v.reshape(n_seq, seg_len, d)
    cum4 = cum_t.reshape(n_seq, heads, nq, tq)
    qblk = pl.BlockSpec((tq, LANES), lambda b, p, i: (b * nq + i, p))
    kblk = pl.BlockSpec((1, seg_len, LANES), lambda b, p, i: (b, 0, p))
    cblk = pl.BlockSpec((1, heads, nq, tq), lambda b, p, i: (b, 0, 0, 0))
    return pl.pallas_call(
        functools.partial(_fox_attn_kernel, tq),
        grid=(n_seq, d // LANES, nq),
        in_specs=[qblk, kblk, kblk, cblk],
        out_specs=qblk, out_shape=jax.ShapeDtypeStruct((n, d), F32),
        scratch_shapes=[pltpu.VMEM((2, tq, LANES), F32), pltpu.VMEM((2, tq, LANES), F32)],
        compiler_params=_cparams(("arbitrary", "arbitrary", "arbitrary")),
    )(q, k3, v3, cum4)


def _fox_out_kernel(o_ref, og_ref, x_ref, w_ref, y_ref):
    y_ref[...] = x_ref[...] + _bdot(o_ref[...] * _sigmoid(og_ref[...]), w_ref[...])


def _fox_out(o, og, x, w, tm):
    n, d = x.shape
    row = pl.BlockSpec((tm, d), lambda i: (i, 0))
    return pl.pallas_call(
        _fox_out_kernel, grid=(n // tm,), in_specs=[row, row, row, _const_spec(w.shape)],
        out_specs=row, out_shape=jax.ShapeDtypeStruct((n, d), F32),
        compiler_params=_cparams(("arbitrary",)),
    )(o, og, x, w)


def _paged_kernel(pps, n_new, *refs):
    pt_ref = refs[0]
    q_ref = refs[1]
    k_refs = refs[2:2 + pps]
    v_refs = refs[2 + pps:2 + 2 * pps]
    lf_refs = refs[2 + 2 * pps:2 + 3 * pps]
    kn_ref, vn_ref, lfn_ref, hsel_ref, tri_ref, o_ref = refs[2 + 3 * pps:8 + 3 * pps]
    qexp, acc, mrun, lrun, crun = refs[8 + 3 * pps:]
    del pt_ref
    step = pl.program_id(1)
    last = pl.num_programs(1) - 1
    rows, d = qexp.shape
    heads = rows // n_new
    rhead = _idiv(lax.broadcasted_iota(jnp.int32, (rows, 1), 0), n_new)
    lhead = _idiv(lax.broadcasted_iota(jnp.int32, (1, d), 1), C_HEAD)

    @pl.when(step == 0)
    def _():
        qf = q_ref[...].astype(F32)
        qt = jnp.concatenate([qf] * heads, axis=0)
        qexp[...] = jnp.where(rhead == lhead, qt, 0.0).astype(BF16)
        acc[...] = jnp.zeros_like(acc)
        mrun[...] = jnp.full_like(mrun, NEG)
        lrun[...] = jnp.zeros_like(lrun)
        crun[...] = jnp.zeros_like(crun)

    def page(kp, vp, lf, new_tokens):
        s = _bdot_nt(qexp[...], kp)
        cum = crun[...]
        for part in _split3(lf):
            spread = lax.dot_general(hsel_ref[...], part, (((1,), (1,)), ((), ())), preferred_element_type=F32)
            cum = cum + jnp.dot(spread.astype(BF16), tri_ref[...], preferred_element_type=F32)
        s = s - cum
        if new_tokens:
            qpos = _pos_in_seq(lax.broadcasted_iota(jnp.int32, (rows, PAGE), 0), n_new)
            kpos = lax.broadcasted_iota(jnp.int32, (rows, PAGE), 1)
            s = jnp.where(kpos <= qpos, s, NEG)
        m_old = mrun[...]
        m_new = jnp.maximum(m_old, jnp.max(s, axis=-1, keepdims=True))
        alpha = jnp.exp(m_old - m_new)
        p = jnp.exp(s - m_new)
        lrun[...] = alpha * lrun[...] + jnp.sum(p, axis=-1, keepdims=True)
        acc[...] = alpha[:, :1] * acc[...] + _bdot(p, vp)
        mrun[...] = m_new
        crun[...] = jnp.broadcast_to(cum[:, PAGE - 1:], crun.shape)

    @pl.when(step < last)
    def _():
        for i in range(pps):
            page(k_refs[i][0, 0], v_refs[i][0, 0], lf_refs[i][0, 0], False)

    @pl.when(step == last)
    def _():
        page(kn_ref[0], vn_ref[0], lfn_ref[0], True)
        res = acc[...] / lrun[...][:, :1]
        out = jnp.zeros((n_new, d), F32)
        for hh in range(heads):
            out = out + jnp.where(lhead == hh, res[hh * n_new:(hh + 1) * n_new, :], 0.0)
        o_ref[...] = out


def _paged_attn(q, k_new, v_new, lf_new, cache_k, cache_v, cache_lf, layer, page_table, n_seq, n_new, pps):
    n, d = q.shape
    heads = d // C_HEAD
    rows = heads * n_new
    n_pages = page_table.shape[1]
    steps = n_pages // pps
    nc, n_pool = cache_k.shape[:2]
    ck = cache_k.reshape(nc, n_pool, PAGE, d)
    cv = cache_v.reshape(nc, n_pool, PAGE, d)
    pad = lambda a: jnp.pad(a.reshape(n_seq, n_new, a.shape[1]), ((0, 0), (0, PAGE - n_new), (0, 0)))
    hsel = (jnp.arange(rows)[:, None] // n_new == jnp.arange(heads)[None, :]).astype(BF16)
    tri = (jnp.arange(PAGE)[:, None] <= jnp.arange(PAGE)[None, :]).astype(BF16)

    def page_spec(i, width):
        return pl.BlockSpec((1, 1, PAGE, width),
                            lambda b, s, pt: (layer, pt[b, jnp.minimum(s, steps - 1) * pps + i], 0, 0))

    new_spec = lambda width: pl.BlockSpec((1, PAGE, width), lambda b, s, pt: (b, 0, 0))
    const = lambda shape: pl.BlockSpec(shape, lambda b, s, pt: (0, 0))
    qspec = pl.BlockSpec((n_new, d), lambda b, s, pt: (b, 0))
    grid_spec = pltpu.PrefetchScalarGridSpec(
        num_scalar_prefetch=1, grid=(n_seq, steps + 1),
        in_specs=[qspec] + [page_spec(i, d) for i in range(pps)] * 2 + [page_spec(i, heads) for i in range(pps)]
        + [new_spec(d), new_spec(d), new_spec(heads), const(hsel.shape), const(tri.shape)],
        out_specs=qspec,
        scratch_shapes=[pltpu.VMEM((rows, d), BF16), pltpu.VMEM((rows, d), F32), pltpu.VMEM((rows, PAGE), F32),
                        pltpu.VMEM((rows, PAGE), F32), pltpu.VMEM((rows, PAGE), F32)])
    return pl.pallas_call(
        functools.partial(_paged_kernel, pps, n_new), grid_spec=grid_spec,
        out_shape=jax.ShapeDtypeStruct((n, d), F32),
        compiler_params=_cparams(("arbitrary", "arbitrary")),
    )(page_table, q, *([ck] * pps), *([cv] * pps), *([cache_lf] * pps), pad(k_new), pad(v_new), pad(lf_new),
      hsel, tri)


def _block_ones(width, block):
    i = jnp.arange(width)
    return (i[:, None] // block == i[None, :] // block).astype(BF16)


def kernel(x_prompt, x_sample, mem_prompt, state_wkv, state_shift, state_conv, state_ffn_conv, cache_k, cache_v, cache_logf, cache_mem_k, cache_mem_v, page_table, norm_mix, norm_cross, norm_mem, norm_ffn, ab_in, ab_out, rw_mu, rw_w0, rw_w2, rw_a0, rw_a2, rw_g2, rw_kk, rw_ka, rw_rk, rw_lnw, rw_lnb, sc_conv_w, fox_in, fox_out, fox_qg, fox_kg, fox_fb, xq_w, xkv_w, xo_w, xq_g, xk_g, ffn_up, ffn_conv_w, ffn_down):
    bp, tp, d = x_prompt.shape
    bs, ts, _ = x_sample.shape
    n_mem = mem_prompt.shape[1]
    depth = norm_mix.shape[0]
    heads_c = d // C_HEAD
    a_width = rw_w0.shape[1]
    b_width = sc_conv_w.shape[2]
    d_ff = ffn_conv_w.shape[2]
    np_, ns_ = bp * tp, bs * ts
    tm_p = min(512, tp)
    row2 = lambda a: a.reshape(1, -1).astype(F32)
    bb = _block_ones(2 * LANES, A_HEAD)
    eye = (jnp.arange(A_HEAD)[:, None] == jnp.arange(a_width)[None, :] % A_HEAD).astype(F32)

    yp = x_prompt.reshape(np_, d)
    ys = x_sample.reshape(ns_, d)
    mem = mem_prompt.reshape(bp * n_mem, d)
    pw, psh, pcv, pfc, pk, pv, plf, pmk, pmv = [], [], [], [], [], [], [], [], []
    sw, ssh, scv, sfc, sk, sv, slf = [], [], [], [], [], [], []

    for layer in range(depth):
        i = layer // 2
        if layer % 2 == 0:
            zrow = jnp.zeros((LANES // 2, a_width), F32)
            prm = dict(
                gm=row2(norm_mix[layer]), win=ab_in[i].astype(BF16), mu=row2(rw_mu[i]), w0=row2(rw_w0[i]),
                w2p=jnp.concatenate([rw_w2[i], zrow], axis=0).astype(BF16), a0=row2(rw_a0[i]),
                a2p=jnp.concatenate([zrow, rw_a2[i]], axis=0).astype(BF16), g2=rw_g2[i].astype(BF16),
                kkw=row2(rw_kk[i]), ka=row2(rw_ka[i]), rk=row2(rw_rk[i]), cw=sc_conv_w[i].astype(F32), bb=bb,
                lnw=row2(rw_lnw[i]), lnb=row2(rw_lnb[i]), wout=ab_out[i].astype(BF16))
            a_proj = ab_in.shape[2] - 3 * b_width
            ops, bonus, g, yb, sh, cv = _ab_in(yp, tp, tm_p, jnp.zeros((bp, a_proj), F32),
                                               jnp.zeros((bp, CONV_W - 1, b_width), F32), prm)
            o, st = _wkv(ops, bp, tp, jnp.zeros((bp, a_width // A_HEAD, A_HEAD, A_HEAD), F32), bb, eye,
                         nb=min(4, bp), tc=min(128, tp))
            yp = _ab_out(o, bonus, g, yb, yp, prm, tm_p)
            pw.append(st); psh.append(sh); pcv.append(cv)
            ops, bonus, g, yb, sh, cv = _ab_in(ys, ts, ns_, state_shift[i], state_conv[i], prm)
            o, st = _wkv(ops, bs, ts, state_wkv[i], bb, eye, nb=min(4, bs), tc=ts)
            ys = _ab_out(o, bonus, g, yb, ys, prm, ns_)
            sw.append(st); ssh.append(sh); scv.append(cv)
        else:
            w = fox_in[i]
            wfl = w[:, 3 * d:3 * d + heads_c]
            prm = dict(
                gm=row2(norm_mix[layer]),
                w=jnp.concatenate([w[:, :3 * d], w[:, 3 * d + heads_c:]], axis=1).astype(BF16),
                wfl=jnp.pad(wfl, ((0, 0), (0, LANES - heads_c))).astype(BF16),
                fb=jnp.pad(row2(fox_fb[i]), ((0, 0), (0, LANES - heads_c))),
                qg=row2(jnp.tile(fox_qg[i], heads_c)), kg=row2(jnp.tile(fox_kg[i], heads_c)), bb=bb,
                wflt=wfl.T.astype(BF16), fbt=fox_fb[i].reshape(heads_c, 1).astype(F32),
                tri=(jnp.arange(tm_p)[:, None] <= jnp.arange(tm_p)[None, :]).astype(BF16))
            wo = fox_out[i].astype(BF16)
            q, k, v, og, lf, cum_t = _fox_in(yp, tp, tm_p, prm, True, BF16)
            o = _fox_attn(q, k, v, cum_t, bp, tp, tm_p)
            yp = _fox_out(o, og, yp, wo, tm_p)
            pk.append(k.reshape(bp, tp, heads_c, C_HEAD)); pv.append(v.reshape(bp, tp, heads_c, C_HEAD))
            plf.append(lf.reshape(bp, tp, heads_c))
            q, k, v, og, lf = _fox_in(ys, ts, ns_, prm, False, F32)
            o = _paged_attn(q, k, v, lf, cache_k, cache_v, cache_logf, i, page_table, bs, ts, pps=4)
            ys = _fox_out(o, og, ys, wo, ns_)
            sk.append(k.reshape(bs, ts, heads_c, C_HEAD)); sv.append(v.reshape(bs, ts, heads_c, C_HEAD))
            slf.append(lf.reshape(bs, ts, heads_c))

        wq, wo_x = xq_w[layer].astype(BF16), xo_w[layer].astype(BF16)
        mk, mv = _mem_kv(mem, row2(norm_mem[layer]), xkv_w[layer].astype(BF16), row2(xk_g[layer]), n_mem)
        pmk.append(mk.reshape(bp, n_mem, X_HEADS, d // X_HEADS)); pmv.append(mv.reshape(bp, n_mem, X_HEADS, d // X_HEADS))
        gx, qg = row2(norm_cross[layer]), row2(xq_g[layer])
        yp = _cross(yp, tp, tm_p, mk.reshape(bp, n_mem, d), mv.reshape(bp, n_mem, d), gx, wq, qg, wo_x)
        ys = _cross(ys, ts, ts, cache_mem_k[layer].reshape(bs, n_mem, d), cache_mem_v[layer].reshape(bs, n_mem, d),
                    gx, wq, qg, wo_x)

        gf, wup, cwf, wdn = row2(norm_ffn[layer]), ffn_up[layer].astype(BF16), ffn_conv_w[layer].astype(F32), \
            ffn_down[layer].astype(BF16)
        yp, cvp = _ffn(yp, tp, tm_p, jnp.zeros((bp, CONV_W - 1, d_ff), F32), gf, wup, cwf, wdn)
        ys, cvs = _ffn(ys, ts, ns_, state_ffn_conv[layer], gf, wup, cwf, wdn)
        pfc.append(cvp); sfc.append(cvs)

    st = jnp.stack
    return (yp.reshape(bp, tp, d), ys.reshape(bs, ts, d), st(pw), st(psh), st(pcv), st(pfc), st(pk), st(pv), st(plf),
            st(pmk), st(pmv), st(sw), st(ssh), st(scv), st(sfc), st(sk), st(sv), st(slf))
```

```python
import functools

import jax
import jax.numpy as jnp
from jax import lax
from jax.experimental import pallas as pl
from jax.experimental.pallas import tpu as pltpu

F32 = jnp.float32
BF16 = jnp.bfloat16

RMS_EPS = 1e-6
GN_EPS = 64e-5
L2_EPS = 1e-12
CONV_W = 3
A_HEAD = 64
C_HEAD = 64
X_HEADS = 4
PAGE = 128
LANES = 128
SUBLANES = 8
HALF = 2 * LANES
VMEM_LIMIT = 56 * 1024 * 1024
NEG = -1e30
LOG2E = 1.4426950408889634


def _bdot(a, b):
    return jnp.dot(a.astype(BF16), b.astype(BF16), preferred_element_type=F32)


def _bdot_nt(a, b):
    return lax.dot_general(a.astype(BF16), b.astype(BF16), (((1,), (1,)), ((), ())), preferred_element_type=F32)


def _split3(x):
    x1 = x.astype(BF16)
    r1 = x - x1.astype(F32)
    x2 = r1.astype(BF16)
    x3 = (r1 - x2.astype(F32)).astype(BF16)
    return x1, x2, x3


def _seg_sum(x, bb):
    w = bb.shape[0]
    outs = []
    for c in range(x.shape[1] // w):
        xc = x[:, c * w:(c + 1) * w]
        hi = xc.astype(BF16)
        lo = (xc - hi.astype(F32)).astype(BF16)
        outs.append(jnp.dot(hi, bb, preferred_element_type=F32) + jnp.dot(lo, bb, preferred_element_type=F32))
    return outs[0] if len(outs) == 1 else jnp.concatenate(outs, axis=1)


def _seg_sum1(xb, bb):
    w = bb.shape[0]
    outs = [jnp.dot(xb[:, c * w:(c + 1) * w], bb, preferred_element_type=F32) for c in range(xb.shape[1] // w)]
    return outs[0] if len(outs) == 1 else jnp.concatenate(outs, axis=1)


def _rms_rows(x, g):
    return x * lax.rsqrt(jnp.mean(x * x, axis=-1, keepdims=True) + RMS_EPS) * g


def _sigmoid(x):
    return 1.0 / (1.0 + jnp.exp(-x))


def _softplus(x):
    return jnp.maximum(x, 0.0) + jnp.log(1.0 + jnp.exp(-jnp.abs(x)))


def _pos_in_seq(rows, seg_len):
    if seg_len & (seg_len - 1) == 0:
        return jnp.bitwise_and(rows, seg_len - 1)
    return lax.rem(rows, seg_len)


def _idiv(x, n):
    if n & (n - 1) == 0:
        return jnp.right_shift(x, n.bit_length() - 1)
    return lax.div(x, n)


def _taps(x, rows, seg_len, tail, e0, e1, want2=True):
    r1 = pltpu.roll(x, 1, 0)
    r2 = pltpu.roll(x, 2, 0) if want2 else None
    if tail is not None:
        c1 = tail[SUBLANES - 1:SUBLANES, :]
        xm1 = jnp.where(rows == 0, c1, r1)
        xm2 = None
        if want2:
            c0 = tail[SUBLANES - 2:SUBLANES - 1, :]
            xm2 = jnp.where(rows == 0, c0, jnp.where(rows == 1, c1, r2))
        return xm1, xm2
    pos = _pos_in_seq(rows, seg_len)
    xm1 = jnp.where(pos == 0, e1, r1)
    xm2 = jnp.where(pos == 0, e0, jnp.where(pos == 1, e1, r2)) if want2 else None
    return xm1, xm2


def _cparams(sem, vmem=VMEM_LIMIT):
    return pltpu.CompilerParams(dimension_semantics=sem, vmem_limit_bytes=vmem)


def _const_spec(shape):
    nd = len(shape)
    return pl.BlockSpec(shape, lambda *_: (0,) * nd, pipeline_mode=pl.Buffered(1))


def _ab_in_kernel(carry_mode, seg_len, tiles_per_seq, a_width, b_width, *refs):
    (x_ref, gm_ref, win_ref, mu_ref, w0_ref, w2_ref, a0_ref, a2_ref, g2_ref, kkw_ref, ka_ref, rk_ref,
     cw_ref, bb_ref) = refs[:14]
    if carry_mode:
        init_pa_ref, init_sq_ref = refs[14:16]
        outs = refs[16:27]
        pa_tail, sq_tail = refs[27:29]
    else:
        e_pa_ref, e_sq0_ref, e_sq1_ref = refs[14:17]
        outs = refs[17:28]
    (r_o, w_o, k_o, v_o, kk_o, b_o, bonus_o, g_o, yb_o, pa_o, sq_o) = outs
    tm = x_ref.shape[0]
    a_proj = 3 * a_width + 2 * LANES
    rows = lax.broadcasted_iota(jnp.int32, (tm, 1), 0)

    if carry_mode:
        @pl.when(pl.program_id(0) % tiles_per_seq == 0)
        def _():
            pa_tail[...] = init_pa_ref[0]
            sq_tail[...] = init_sq_ref[0]

    x = x_ref[...]
    h = _rms_rows(x, gm_ref[...]).astype(BF16)
    proj = jnp.dot(h, win_ref[...], preferred_element_type=F32)
    pa = proj[:, :a_proj]
    pb = proj[:, a_proj:]

    if carry_mode:
        prev, _ = _taps(pa, rows, seg_len, pa_tail[...], None, None, want2=False)
    else:
        prev, _ = _taps(pa, rows, seg_len, None, None, e_pa_ref[...], want2=False)
    xs = pa + (prev - pa) * mu_ref[...]
    r = xs[:, :a_width]
    k = xs[:, a_width:2 * a_width]
    v = xs[:, 2 * a_width:3 * a_width]
    wa = xs[:, 3 * a_width:3 * a_width + LANES]
    gd = xs[:, 3 * a_width + LANES:a_proj]
    wl = w0_ref[...] + _bdot(jnp.tanh(wa), w2_ref[...])
    w = -_softplus(-wl) - 0.5
    decay = jnp.exp(-jnp.exp(w))
    a = _sigmoid(a0_ref[...] + _bdot(wa, a2_ref[...]))
    g = _bdot(_sigmoid(gd), g2_ref[...])
    bb = bb_ref[...]
    kk = k * kkw_ref[...]
    kk = kk / jnp.maximum(jnp.sqrt(_seg_sum(kk * kk, bb)), L2_EPS)
    k2 = k * (1.0 + (a - 1.0) * ka_ref[...])
    bonus = _seg_sum(r * k2 * rk_ref[...], bb) * v
    r_o[...] = r
    w_o[...] = decay
    k_o[...] = k2
    v_o[...] = v
    kk_o[...] = kk
    b_o[...] = kk * a
    bonus_o[...] = bonus
    g_o[...] = g

    gb = pb[:, :b_width]
    sq = pb[:, b_width:2 * b_width] * pb[:, 2 * b_width:]
    if carry_mode:
        xm1, xm2 = _taps(sq, rows, seg_len, sq_tail[...], None, None)
    else:
        xm1, xm2 = _taps(sq, rows, seg_len, None, e_sq0_ref[...], e_sq1_ref[...])
    cw = cw_ref[...]
    yb_o[...] = gb * (cw[0:1, :] * xm2 + cw[1:2, :] * xm1 + cw[2:3, :] * sq)

    if carry_mode:
        pa_last = pa[tm - SUBLANES:, :]
        sq_last = sq[tm - SUBLANES:, :]
        pa_tail[...] = pa_last
        sq_tail[...] = sq_last
        pa_o[0] = pa_last
        sq_o[0] = sq_last
    else:
        pa_o[...] = pa
        sq_o[...] = sq


def _tail8(state):
    b, n, c = state.shape
    return jnp.concatenate([jnp.zeros((b, SUBLANES - n, c), state.dtype), state], axis=1)


def _ab_in(x, seg_len, tm, shift_prev, conv_prev, prm):
    n, d = x.shape
    n_seq = n // seg_len
    a_width = prm["w0"].shape[1]
    b_width = prm["cw"].shape[1]
    a_proj = 3 * a_width + 2 * LANES
    carry_mode = seg_len >= tm
    tiles_per_seq = max(seg_len // tm, 1)
    grid = (n // tm,)
    row = lambda c: pl.BlockSpec((tm, c), lambda i: (i, 0))
    params = [prm["gm"], prm["win"], prm["mu"], prm["w0"], prm["w2p"], prm["a0"], prm["a2p"], prm["g2"],
              prm["kkw"], prm["ka"], prm["rk"], prm["cw"], prm["bb"]]
    in_specs = [row(d)] + [_const_spec(p.shape) for p in params]
    out_shape = [jax.ShapeDtypeStruct((n, a_width), F32)] * 8 + [jax.ShapeDtypeStruct((n, b_width), F32)]
    out_specs = [row(a_width)] * 8 + [row(b_width)]
    if carry_mode:
        seq_blk = lambda c: pl.BlockSpec((1, SUBLANES, c), lambda i: (i // tiles_per_seq, 0, 0))
        extra = [_tail8(shift_prev[:, None, :]), _tail8(conv_prev)]
        in_specs += [seq_blk(a_proj), seq_blk(b_width)]
        out_shape += [jax.ShapeDtypeStruct((n_seq, SUBLANES, a_proj), F32),
                      jax.ShapeDtypeStruct((n_seq, SUBLANES, b_width), F32)]
        out_specs += [seq_blk(a_proj), seq_blk(b_width)]
        scratch = [pltpu.VMEM((SUBLANES, a_proj), F32), pltpu.VMEM((SUBLANES, b_width), F32)]
    else:
        extra = [jnp.repeat(shift_prev, seg_len, axis=0), jnp.repeat(conv_prev[:, 0], seg_len, axis=0),
                 jnp.repeat(conv_prev[:, 1], seg_len, axis=0)]
        in_specs += [row(a_proj), row(b_width), row(b_width)]
        out_shape += [jax.ShapeDtypeStruct((n, a_proj), F32), jax.ShapeDtypeStruct((n, b_width), F32)]
        out_specs += [row(a_proj), row(b_width)]
        scratch = []
    outs = pl.pallas_call(
        functools.partial(_ab_in_kernel, carry_mode, seg_len, tiles_per_seq, a_width, b_width),
        grid=grid, in_specs=in_specs, out_specs=out_specs, out_shape=out_shape, scratch_shapes=scratch,
        compiler_params=_cparams(("arbitrary",)),
    )(x, *params, *extra)
    scan_ops = outs[:6]
    bonus, g, yb, pa_t, sq_t = outs[6:]
    if carry_mode:
        shift_new = pa_t[:, SUBLANES - 1]
        conv_new = sq_t[:, SUBLANES - (CONV_W - 1):]
    else:
        shift_new = pa_t.reshape(n_seq, seg_len, a_proj)[:, -1]
        conv_new = sq_t.reshape(n_seq, seg_len, b_width)[:, -(CONV_W - 1):]
    return scan_ops, bonus, g, yb, shift_new, conv_new


def _wkv_kernel(unroll, r_ref, w_ref, k_ref, v_ref, kk_ref, b_ref, s0_ref, bb_ref, eye_ref, hmask_ref,
                o_ref, s_out_ref, s_scr, s_bf, lhs, res):
    nb, tc, width = r_ref.shape
    nh = width // HALF
    heads = width // A_HEAD
    rc = 2 * SUBLANES
    pieces = [(c, h) for c in range(A_HEAD // rc) for h in range(nh)]
    bb = bb_ref[...]

    def rows_of(c):
        return slice(c * rc, (c + 1) * rc)

    def stage_rows(h, kind, c):
        base = (h * 2 + kind) * A_HEAD + c * rc
        return slice(base, base + rc)

    def build(b, t):
        kk, v = kk_ref[b, pl.ds(t, 1), :], v_ref[b, pl.ds(t, 1), :]
        for c, h in pieces:
            ls = slice(h * HALF, (h + 1) * HALF)
            lhs[b, stage_rows(h, 0, c), :] = (s_scr[b, rows_of(c), ls] * kk[:, ls]).astype(BF16)
            lhs[b, stage_rows(h, 1, c), :] = (eye_ref[rows_of(c), :] * v[:, ls]).astype(BF16)
        res[b] = jnp.dot(lhs[b], bb, preferred_element_type=F32)

    def update(b, t):
        row = lambda ref: ref[b, pl.ds(t, 1), :]
        w, bv, k, r = row(w_ref), row(b_ref), row(k_ref), row(r_ref)
        for c, h in pieces:
            ls = slice(h * HALF, (h + 1) * HALF)
            s = (s_scr[b, rows_of(c), ls] * w[:, ls] - res[b, stage_rows(h, 0, c), :] * bv[:, ls]
                 + res[b, stage_rows(h, 1, c), :] * k[:, ls])
            s_scr[b, rows_of(c), ls] = s
            s_bf[b, rows_of(c), ls] = s.astype(BF16)
        rmat = (hmask_ref[...] * r).astype(BF16)
        y = _bdot_nt(rmat, s_bf[b])
        o_ref[b, pl.ds(t, 1)] = y[:heads][None]

    @pl.when(pl.program_id(1) == 0)
    def _():
        s_scr[...] = s0_ref[...]

    for b in range(nb):
        build(b, 0)

    def step(t, carry):
        nxt = jnp.minimum(t + 1, tc - 1)
        for b in range(nb):
            update(b, t)
            build(b, nxt)
        return carry

    lax.fori_loop(0, tc, step, 0, unroll=unroll)

    @pl.when(pl.program_id(1) == pl.num_programs(1) - 1)
    def _():
        s_out_ref[...] = s_scr[...]


def _wkv(scan_ops, n_seq, seg_len, s0, bb, nb, tc, unroll):
    width = scan_ops[0].shape[1]
    heads = width // A_HEAD
    nh = width // HALF
    ops3 = [a.reshape(n_seq, seg_len, width) for a in scan_ops]
    s0_l = jnp.transpose(s0.astype(F32), (0, 2, 1, 3)).reshape(n_seq, A_HEAD, width)
    eye = (jnp.arange(A_HEAD)[:, None] == jnp.arange(HALF)[None, :] % A_HEAD).astype(F32)
    hrows = -(-heads // (2 * SUBLANES)) * 2 * SUBLANES
    hmask = (jnp.arange(hrows)[:, None] == jnp.arange(width)[None, :] // A_HEAD).astype(F32)
    seq_blk = pl.BlockSpec((nb, tc, width), lambda b, c: (b, c, 0))
    st_blk = pl.BlockSpec((nb, A_HEAD, width), lambda b, c: (b, 0, 0))
    stage = (nb, 2 * nh * A_HEAD, HALF)
    o, s_new = pl.pallas_call(
        functools.partial(_wkv_kernel, unroll), grid=(n_seq // nb, seg_len // tc),
        in_specs=[seq_blk] * 6 + [st_blk, _const_spec(bb.shape), _const_spec(eye.shape), _const_spec(hmask.shape)],
        out_specs=[pl.BlockSpec((nb, tc, heads, A_HEAD), lambda b, c: (b, c, 0, 0)), st_blk],
        out_shape=[jax.ShapeDtypeStruct((n_seq, seg_len, heads, A_HEAD), F32),
                   jax.ShapeDtypeStruct((n_seq, A_HEAD, width), F32)],
        scratch_shapes=[pltpu.VMEM((nb, A_HEAD, width), F32), pltpu.VMEM((nb, A_HEAD, width), BF16),
                        pltpu.VMEM(stage, BF16), pltpu.VMEM(stage, F32)],
        compiler_params=_cparams(("arbitrary", "arbitrary")),
    )(*ops3, s0_l, bb, eye, hmask)
    s_new = jnp.transpose(s_new.reshape(n_seq, A_HEAD, heads, A_HEAD), (0, 2, 1, 3))
    return o.reshape(n_seq * seg_len, width), s_new


def _ab_out_kernel(o_ref, bonus_ref, g_ref, yb_ref, x_ref, lnw_ref, lnb_ref, wout_ref, bb_ref, y_ref):
    a_width = o_ref.shape[1]
    o = o_ref[...]
    bb = bb_ref[...]
    mean = _seg_sum(o, bb) * (1.0 / A_HEAD)
    dlt = o - mean
    var = _seg_sum(dlt * dlt, bb) * (1.0 / A_HEAD)
    yn = dlt * lax.rsqrt(var + GN_EPS) * lnw_ref[...] + lnb_ref[...]
    ya = (yn + bonus_ref[...]) * g_ref[...]
    y = _bdot(ya, wout_ref[:a_width, :]) + _bdot(yb_ref[...], wout_ref[a_width:, :])
    y_ref[...] = x_ref[...] + y


def _ab_out(o, bonus, g, yb, x, prm, tm):
    n, d = x.shape
    row = lambda c: pl.BlockSpec((tm, c), lambda i: (i, 0))
    params = [prm["lnw"], prm["lnb"], prm["wout"], prm["bb"]]
    return pl.pallas_call(
        _ab_out_kernel, grid=(n // tm,),
        in_specs=[row(o.shape[1]), row(bonus.shape[1]), row(g.shape[1]), row(yb.shape[1]), row(d)]
        + [_const_spec(p.shape) for p in params],
        out_specs=row(d), out_shape=jax.ShapeDtypeStruct((n, d), F32),
        compiler_params=_cparams(("arbitrary",)),
    )(o, bonus, g, yb, x, *params)


def _mem_kv_kernel(x_ref, gm_ref, wkv_ref, kg_ref, k_ref, v_ref):
    d = x_ref.shape[1]
    hd = d // X_HEADS
    h = _rms_rows(x_ref[...], gm_ref[...])
    kv = _bdot(h, wkv_ref[...])
    kg = kg_ref[...]
    for hh in range(X_HEADS):
        kh = kv[:, hh * hd:(hh + 1) * hd]
        k_ref[:, hh * hd:(hh + 1) * hd] = _rms_rows(kh, kg)
    v_ref[...] = kv[:, d:]


def _mem_kv(mem, gm, wkv, kg, tm):
    n, d = mem.shape
    row = pl.BlockSpec((tm, d), lambda i: (i, 0))
    return pl.pallas_call(
        _mem_kv_kernel, grid=(n // tm,),
        in_specs=[row, _const_spec(gm.shape), _const_spec(wkv.shape), _const_spec(kg.shape)],
        out_specs=[row, row], out_shape=[jax.ShapeDtypeStruct((n, d), F32)] * 2,
        compiler_params=_cparams(("arbitrary",)),
    )(mem, gm, wkv, kg)


def _cross_kernel(x_ref, mk_ref, mv_ref, gm_ref, wq_ref, qg_ref, wo_ref, y_ref, o_scr):
    d = x_ref.shape[1]
    hd = d // X_HEADS
    x = x_ref[...]
    q = _bdot(_rms_rows(x, gm_ref[...]), wq_ref[...])
    qg = qg_ref[...]
    scale = hd ** -0.5
    for hh in range(X_HEADS):
        sl = slice(hh * hd, (hh + 1) * hd)
        qh = _rms_rows(q[:, sl], qg)
        s = _bdot_nt(qh, mk_ref[0, :, sl]) * scale
        p = jnp.exp(s - jnp.max(s, axis=-1, keepdims=True))
        p = p / jnp.sum(p, axis=-1, keepdims=True)
        o_scr[:, sl] = _bdot(p, mv_ref[0, :, sl])
    y_ref[...] = x + _bdot(o_scr[...], wo_ref[...])


def _cross(x, seg_len, tm, mk, mv, gm, wq, qg, wo):
    n, d = x.shape
    tiles_per_seq = seg_len // tm
    row = pl.BlockSpec((tm, d), lambda i: (i, 0))
    mem = pl.BlockSpec((1, mk.shape[1], d), lambda i: (i // tiles_per_seq, 0, 0))
    params = [gm, wq, qg, wo]
    return pl.pallas_call(
        _cross_kernel, grid=(n // tm,),
        in_specs=[row, mem, mem] + [_const_spec(p.shape) for p in params],
        out_specs=row, out_shape=jax.ShapeDtypeStruct((n, d), F32),
        scratch_shapes=[pltpu.VMEM((tm, d), F32)],
        compiler_params=_cparams(("arbitrary",)),
    )(x, mk, mv, *params)


def _ffn_kernel(carry_mode, seg_len, tiles_per_seq, n_chunks, *refs):
    x_ref, gm_ref, wup_ref, cw_ref, wdn_ref = refs[:5]
    if carry_mode:
        init_ref, y_ref, gate_o, tail = refs[5:9]
    else:
        e0_ref, e1_ref, y_ref, gate_o = refs[5:9]
    tm = x_ref.shape[0]
    d_ff = cw_ref.shape[1]
    ck = d_ff // n_chunks
    rows = lax.broadcasted_iota(jnp.int32, (tm, 1), 0)

    if carry_mode:
        @pl.when(pl.program_id(0) % tiles_per_seq == 0)
        def _():
            tail[...] = init_ref[0]

    x = x_ref[...]
    h = _rms_rows(x, gm_ref[...]).astype(BF16)
    acc = x
    for c in range(n_chunks):
        sl = slice(c * ck, (c + 1) * ck)
        gate = jnp.dot(h, wup_ref[:, sl], preferred_element_type=F32)
        up = jnp.dot(h, wup_ref[:, d_ff + c * ck:d_ff + (c + 1) * ck], preferred_element_type=F32)
        if carry_mode:
            xm1, xm2 = _taps(gate, rows, seg_len, tail[:, sl], None, None)
        else:
            xm1, xm2 = _taps(gate, rows, seg_len, None, e0_ref[:, sl], e1_ref[:, sl])
        cv = cw_ref[0:1, sl] * xm2 + cw_ref[1:2, sl] * xm1 + cw_ref[2:3, sl] * gate
        act = cv * _sigmoid(cv) * up
        acc = acc + _bdot(act, wdn_ref[sl, :])
        if carry_mode:
            last = gate[tm - SUBLANES:, :]
            tail[:, sl] = last
            gate_o[0, :, sl] = last
        else:
            gate_o[:, sl] = gate
    y_ref[...] = acc


def _ffn(x, seg_len, tm, conv_prev, gm, wup, cw, wdn, n_chunks=2):
    n, d = x.shape
    d_ff = cw.shape[1]
    n_seq = n // seg_len
    carry_mode = seg_len >= tm
    tiles_per_seq = max(seg_len // tm, 1)
    row = lambda c: pl.BlockSpec((tm, c), lambda i: (i, 0))
    params = [gm, wup, cw, wdn]
    in_specs = [row(d)] + [_const_spec(p.shape) for p in params]
    if carry_mode:
        seq_blk = pl.BlockSpec((1, SUBLANES, d_ff), lambda i: (i // tiles_per_seq, 0, 0))
        extra = [_tail8(conv_prev)]
        in_specs += [seq_blk]
        out_specs = [row(d), seq_blk]
        out_shape = [jax.ShapeDtypeStruct((n, d), F32), jax.ShapeDtypeStruct((n_seq, SUBLANES, d_ff), F32)]
        scratch = [pltpu.VMEM((SUBLANES, d_ff), F32)]
    else:
        extra = [jnp.repeat(conv_prev[:, 0], seg_len, axis=0), jnp.repeat(conv_prev[:, 1], seg_len, axis=0)]
        in_specs += [row(d_ff), row(d_ff)]
        out_specs = [row(d), row(d_ff)]
        out_shape = [jax.ShapeDtypeStruct((n, d), F32), jax.ShapeDtypeStruct((n, d_ff), F32)]
        scratch = []
    y, gate_t = pl.pallas_call(
        functools.partial(_ffn_kernel, carry_mode, seg_len, tiles_per_seq, n_chunks),
        grid=(n // tm,), in_specs=in_specs, out_specs=out_specs, out_shape=out_shape, scratch_shapes=scratch,
        compiler_params=_cparams(("arbitrary",)),
    )(x, *params, *extra)
    if carry_mode:
        conv_new = gate_t[:, SUBLANES - (CONV_W - 1):]
    else:
        conv_new = gate_t.reshape(n_seq, seg_len, d_ff)[:, -(CONV_W - 1):]
    return y, conv_new


def _log_sigmoid(x):
    return jnp.minimum(x, 0.0) - jnp.log(1.0 + jnp.exp(-jnp.abs(x)))


def _fox_in_kernel(with_cum, tiles_per_seq, *refs):
    x_ref, gm_ref, w_ref, wfl_ref, fb_ref, qg_ref, kg_ref, bb_ref = refs[:8]
    if with_cum:
        wflt_ref, fbt_ref, tri_ref = refs[8:11]
        q_o, k_o, v_o, og_o, lf_o, cum_o, run = refs[11:18]
    else:
        q_o, k_o, v_o, og_o, lf_o = refs[8:13]
    d = x_ref.shape[1]
    heads = lf_o.shape[1]
    h = _rms_rows(x_ref[...], gm_ref[...]).astype(BF16)
    proj = jnp.dot(h, w_ref[...], preferred_element_type=F32)
    bb = bb_ref[...]
    inv = 1.0 / C_HEAD

    def head_norm(z, gain):
        return z * lax.rsqrt(_seg_sum(z * z, bb) * inv + RMS_EPS) * gain

    q_o[...] = (head_norm(proj[:, :d], qg_ref[...]) * (C_HEAD ** -0.5 * LOG2E)).astype(q_o.dtype)
    k_o[...] = head_norm(proj[:, d:2 * d], kg_ref[...])
    v_o[...] = proj[:, 2 * d:3 * d]
    og_o[...] = proj[:, 3 * d:]
    fl = jnp.dot(h, wfl_ref[...], preferred_element_type=F32) + fb_ref[...]
    lf_o[...] = _log_sigmoid(fl)[:, :heads]

    if with_cum:
        @pl.when(pl.program_id(0) % tiles_per_seq == 0)
        def _():
            run[...] = jnp.zeros_like(run)

        flt = lax.dot_general(wflt_ref[...], h, (((1,), (1,)), ((), ())), preferred_element_type=F32)
        lft = _log_sigmoid(flt + fbt_ref[...])
        tri = tri_ref[...]
        cum = run[...][:, :1]
        for part in _split3(lft):
            cum = cum + jnp.dot(part, tri, preferred_element_type=F32)
        cum_o[0] = cum
        run[...] = jnp.broadcast_to(cum[:, cum.shape[1] - 1:], run.shape)


def _fox_in(x, seg_len, tm, prm, with_cum, q_dtype):
    n, d = x.shape
    n_seq = n // seg_len
    heads = d // C_HEAD
    tiles_per_seq = max(seg_len // tm, 1)
    row = lambda c: pl.BlockSpec((tm, c), lambda i: (i, 0))
    params = [prm["gm"], prm["w"], prm["wfl"], prm["fb"], prm["qg"], prm["kg"], prm["bb"]]
    out_shape = [jax.ShapeDtypeStruct((n, d), q_dtype)] + [jax.ShapeDtypeStruct((n, d), F32)] * 3 + \
                [jax.ShapeDtypeStruct((n, heads), F32)]
    out_specs = [row(d)] * 4 + [row(heads)]
    scratch = []
    if with_cum:
        params += [prm["wflt"], prm["fbt"], prm["tri"]]
        out_shape += [jax.ShapeDtypeStruct((n_seq, heads, seg_len), F32)]
        out_specs += [pl.BlockSpec((1, heads, tm), lambda i: (i // tiles_per_seq, 0, i % tiles_per_seq))]
        scratch = [pltpu.VMEM((heads, LANES), F32)]
    return pl.pallas_call(
        functools.partial(_fox_in_kernel, with_cum, tiles_per_seq),
        grid=(n // tm,), in_specs=[row(d)] + [_const_spec(p.shape) for p in params],
        out_specs=out_specs, out_shape=out_shape, scratch_shapes=scratch,
        compiler_params=_cparams(("arbitrary",)),
    )(x, *params)


def _fox_attn_kernel(tk, q_ref, k_ref, v_ref, cum_ref, o_ref, acc, mrun, s_a, s_b):
    tq = q_ref.shape[0]
    pair = pl.program_id(1)
    qi = pl.program_id(2)
    lane = lax.broadcasted_iota(jnp.int32, (1, LANES), 1)
    first = lane < C_HEAD
    q = q_ref[...]
    zero = jnp.zeros_like(q)
    qs = (jnp.where(first, q, zero), jnp.where(first, zero, q))
    acc[...] = jnp.zeros_like(acc)
    mrun[...] = jnp.full_like(mrun, NEG)

    def qk(j, s_ref):
        kb = k_ref[0, pl.ds(j * tk, tk), :].astype(BF16)
        for hh in range(2):
            s_ref[hh] = _bdot_nt(qs[hh], kb)

    def softmax_pv(j, s_ref, masked):
        vb = v_ref[0, pl.ds(j * tk, tk), :].astype(BF16)
        one = jnp.ones_like(vb)
        vs = (jnp.where(first, vb, one), jnp.where(first, one, vb))
        for hh in range(2):
            bias = cum_ref[0, 2 * pair + hh, pl.ds(j, 1), :] * (-LOG2E)
            s = s_ref[hh] + bias
            if masked:
                rr = lax.broadcasted_iota(jnp.int32, (tq, tk), 0)
                cc = lax.broadcasted_iota(jnp.int32, (tq, tk), 1)
                s = jnp.where(cc <= rr, s, NEG)
            m_old = mrun[hh]
            m_new = jnp.maximum(m_old, jnp.max(s, axis=-1, keepdims=True))
            p = jnp.exp2(s - m_new[:, :1])
            acc[hh] = jnp.exp2(m_old - m_new) * acc[hh] + jnp.dot(p.astype(BF16), vs[hh],
                                                                  preferred_element_type=F32)
            mrun[hh] = m_new

    pairs = qi // 2
    qk(0, s_a)

    def body(jj, carry):
        qk(2 * jj + 1, s_b)
        softmax_pv(2 * jj, s_a, False)
        qk(2 * jj + 2, s_a)
        softmax_pv(2 * jj + 1, s_b, False)
        return carry

    lax.fori_loop(0, pairs, body, 0)

    @pl.when(qi == 2 * pairs)
    def _():
        softmax_pv(qi, s_a, True)

    @pl.when(qi != 2 * pairs)
    def _():
        qk(qi, s_b)
        softmax_pv(qi - 1, s_a, False)
        softmax_pv(qi, s_b, True)

    a0 = acc[0]
    a1 = acc[1]
    o_ref[...] = jnp.where(first, a0 / pltpu.roll(a0, C_HEAD, 1), a1 / pltpu.roll(a1, C_HEAD, 1))


def _fox_attn(q, k, v, cum_t, n_seq, seg_len, tq):
    n, d = q.shape
    heads = d // C_HEAD
    nq = seg_len // tq
    k3 = k.reshape(n_seq, seg_len, d)
    v3 = v.reshape(n_seq, seg_len, d)
    cum4 = cum_t.reshape(n_seq, heads, nq, tq)
    qblk = pl.BlockSpec((tq, LANES), lambda b, p, i: (b * nq + i, p))
    kblk = pl.BlockSpec((1, seg_len, LANES), lambda b, p, i: (b, 0, p))
    cblk = pl.BlockSpec((1, heads, nq, tq), lambda b, p, i: (b, 0, 0, 0))
    return pl.pallas_call(
        functools.partial(_fox_attn_kernel, tq),
        grid=(n_seq, d // LANES, nq),
        in_specs=[qblk, kblk, kblk, cblk],
        out_specs=qblk, out_shape=jax.ShapeDtypeStruct((n, d), F32),
        scratch_shapes=[pltpu.VMEM((2, tq, LANES), F32), pltpu.VMEM((2, tq, LANES), F32),
                        pltpu.VMEM((2, tq, tq), F32), pltpu.VMEM((2, tq, tq), F32)],
        compiler_params=_cparams(("arbitrary", "arbitrary", "arbitrary")),
    )(q, k3, v3, cum4)


def _fox_out_kernel(o_ref, og_ref, x_ref, w_ref, y_ref):
    y_ref[...] = x_ref[...] + _bdot(o_ref[...] * _sigmoid(og_ref[...]), w_ref[...])


def _fox_out(o, og, x, w, tm):
    n, d = x.shape
    row = pl.BlockSpec((tm, d), lambda i: (i, 0))
    return pl.pallas_call(
        _fox_out_kernel, grid=(n // tm,), in_specs=[row, row, row, _const_spec(w.shape)],
        out_specs=row, out_shape=jax.ShapeDtypeStruct((n, d), F32),
        compiler_params=_cparams(("arbitrary",)),
    )(o, og, x, w)


def _paged_kernel(pps, n_new, *refs):
    pt_ref = refs[0]
    q_ref = refs[1]
    k_refs = refs[2:2 + pps]
    v_refs = refs[2 + pps:2 + 2 * pps]
    lf_refs = refs[2 + 2 * pps:2 + 3 * pps]
    kn_ref, vn_ref, lfn_ref, hsel_ref, tri_ref, o_ref = refs[2 + 3 * pps:8 + 3 * pps]
    qexp, acc, mrun, lrun, crun = refs[8 + 3 * pps:]
    del pt_ref
    step = pl.program_id(1)
    last = pl.num_programs(1) - 1
    rows, d = qexp.shape
    heads = rows // n_new
    rhead = _idiv(lax.broadcasted_iota(jnp.int32, (rows, 1), 0), n_new)
    lhead = _idiv(lax.broadcasted_iota(jnp.int32, (1, d), 1), C_HEAD)

    @pl.when(step == 0)
    def _():
        qf = q_ref[...].astype(F32)
        qt = jnp.concatenate([qf] * heads, axis=0)
        qexp[...] = jnp.where(rhead == lhead, qt, 0.0).astype(BF16)
        acc[...] = jnp.zeros_like(acc)
        mrun[...] = jnp.full_like(mrun, NEG)
        lrun[...] = jnp.zeros_like(lrun)
        crun[...] = jnp.zeros_like(crun)

    def page(kt, vt, lft, new_tokens):
        s = _bdot(qexp[...], kt)
        cum = crun[...]
        for part in _split3(lft):
            spread = jnp.dot(hsel_ref[...], part, preferred_element_type=F32)
            cum = cum + jnp.dot(spread.astype(BF16), tri_ref[...], preferred_element_type=F32)
        s = s - cum * LOG2E
        if new_tokens:
            qpos = _pos_in_seq(lax.broadcasted_iota(jnp.int32, (rows, PAGE), 0), n_new)
            kpos = lax.broadcasted_iota(jnp.int32, (rows, PAGE), 1)
            s = jnp.where(kpos <= qpos, s, NEG)
        m_old = mrun[...]
        m_new = jnp.maximum(m_old, jnp.max(s, axis=-1, keepdims=True))
        alpha = jnp.exp2(m_old - m_new)
        p = jnp.exp2(s - m_new)
        lrun[...] = alpha * lrun[...] + jnp.sum(p, axis=-1, keepdims=True)
        acc[...] = alpha[:, :1] * acc[...] + _bdot_nt(p, vt)
        mrun[...] = m_new
        crun[...] = jnp.broadcast_to(cum[:, PAGE - 1:], crun.shape)

    @pl.when(step < last)
    def _():
        for i in range(pps):
            page(k_refs[i][0, 0], v_refs[i][0, 0], lf_refs[i][0, 0], False)

    @pl.when(step == last)
    def _():
        page(kn_ref[0], vn_ref[0], lfn_ref[0], True)
        res = acc[...] / lrun[...][:, :1]
        out = jnp.zeros((n_new, d), F32)
        for hh in range(heads):
            out = out + jnp.where(lhead == hh, res[hh * n_new:(hh + 1) * n_new, :], 0.0)
        o_ref[...] = out


def _paged_attn(q, k_new, v_new, lf_new, cache_k, cache_v, cache_lf, layer, page_table, n_seq, n_new, pps):
    n, d = q.shape
    heads = d // C_HEAD
    rows = heads * n_new
    n_pages = page_table.shape[1]
    steps = n_pages // pps
    nc, n_pool = cache_k.shape[:2]
    ck = jnp.transpose(cache_k, (0, 1, 3, 4, 2)).reshape(nc, n_pool, d, PAGE)
    cv = jnp.transpose(cache_v, (0, 1, 3, 4, 2)).reshape(nc, n_pool, d, PAGE)
    cache_lf = jnp.transpose(cache_lf, (0, 1, 3, 2))
    pad = lambda a: jnp.pad(jnp.transpose(a.reshape(n_seq, n_new, a.shape[1]), (0, 2, 1)),
                            ((0, 0), (0, 0), (0, PAGE - n_new)))
    hsel = (jnp.arange(rows)[:, None] // n_new == jnp.arange(heads)[None, :]).astype(BF16)
    tri = (jnp.arange(PAGE)[:, None] <= jnp.arange(PAGE)[None, :]).astype(BF16)

    def page_spec(i, width):
        return pl.BlockSpec((1, 1, width, PAGE),
                            lambda b, s, pt: (layer, pt[b, jnp.minimum(s, steps - 1) * pps + i], 0, 0))

    new_spec = lambda width: pl.BlockSpec((1, width, PAGE), lambda b, s, pt: (b, 0, 0))
    const = lambda shape: pl.BlockSpec(shape, lambda b, s, pt: (0, 0))
    qspec = pl.BlockSpec((n_new, d), lambda b, s, pt: (b, 0))
    grid_spec = pltpu.PrefetchScalarGridSpec(
        num_scalar_prefetch=1, grid=(n_seq, steps + 1),
        in_specs=[qspec] + [page_spec(i, d) for i in range(pps)] * 2 + [page_spec(i, heads) for i in range(pps)]
        + [new_spec(d), new_spec(d), new_spec(heads), const(hsel.shape), const(tri.shape)],
        out_specs=qspec,
        scratch_shapes=[pltpu.VMEM((rows, d), BF16), pltpu.VMEM((rows, d), F32), pltpu.VMEM((rows, PAGE), F32),
                        pltpu.VMEM((rows, PAGE), F32), pltpu.VMEM((rows, PAGE), F32)])
    return pl.pallas_call(
        functools.partial(_paged_kernel, pps, n_new), grid_spec=grid_spec,
        out_shape=jax.ShapeDtypeStruct((n, d), F32),
        compiler_params=_cparams(("arbitrary", "arbitrary")),
    )(page_table, q, *([ck] * pps), *([cv] * pps), *([cache_lf] * pps), pad(k_new), pad(v_new), pad(lf_new),
      hsel, tri)


def _block_ones(width, block):
    i = jnp.arange(width)
    return (i[:, None] // block == i[None, :] // block).astype(BF16)


def kernel(x_prompt, x_sample, mem_prompt, state_wkv, state_shift, state_conv, state_ffn_conv, cache_k, cache_v, cache_logf, cache_mem_k, cache_mem_v, page_table, norm_mix, norm_cross, norm_mem, norm_ffn, ab_in, ab_out, rw_mu, rw_w0, rw_w2, rw_a0, rw_a2, rw_g2, rw_kk, rw_ka, rw_rk, rw_lnw, rw_lnb, sc_conv_w, fox_in, fox_out, fox_qg, fox_kg, fox_fb, xq_w, xkv_w, xo_w, xq_g, xk_g, ffn_up, ffn_conv_w, ffn_down):
    bp, tp, d = x_prompt.shape
    bs, ts, _ = x_sample.shape
    n_mem = mem_prompt.shape[1]
    depth = norm_mix.shape[0]
    heads_c = d // C_HEAD
    a_width = rw_w0.shape[1]
    b_width = sc_conv_w.shape[2]
    d_ff = ffn_conv_w.shape[2]
    np_, ns_ = bp * tp, bs * ts
    tm_p = min(512, tp)
    row2 = lambda a: a.reshape(1, -1).astype(F32)
    bb = _block_ones(HALF, A_HEAD)

    yp = x_prompt.reshape(np_, d)
    ys = x_sample.reshape(ns_, d)
    mem = mem_prompt.reshape(bp * n_mem, d)
    pw, psh, pcv, pfc, pk, pv, plf, pmk, pmv = [], [], [], [], [], [], [], [], []
    sw, ssh, scv, sfc, sk, sv, slf = [], [], [], [], [], [], []

    for layer in range(depth):
        i = layer // 2
        if layer % 2 == 0:
            zrow = jnp.zeros((LANES // 2, a_width), F32)
            prm = dict(
                gm=row2(norm_mix[layer]), win=ab_in[i].astype(BF16), mu=row2(rw_mu[i]), w0=row2(rw_w0[i]),
                w2p=jnp.concatenate([rw_w2[i], zrow], axis=0).astype(BF16), a0=row2(rw_a0[i]),
                a2p=jnp.concatenate([zrow, rw_a2[i]], axis=0).astype(BF16), g2=rw_g2[i].astype(BF16),
                kkw=row2(rw_kk[i]), ka=row2(rw_ka[i]), rk=row2(rw_rk[i]), cw=sc_conv_w[i].astype(F32), bb=bb,
                lnw=row2(rw_lnw[i]), lnb=row2(rw_lnb[i]), wout=ab_out[i].astype(BF16))
            a_proj = ab_in.shape[2] - 3 * b_width
            ops, bonus, g, yb, sh, cv = _ab_in(yp, tp, tm_p, jnp.zeros((bp, a_proj), F32),
                                               jnp.zeros((bp, CONV_W - 1, b_width), F32), prm)
            o, st = _wkv(ops, bp, tp, jnp.zeros((bp, a_width // A_HEAD, A_HEAD, A_HEAD), F32), bb,
                         nb=min(4, bp), tc=min(128, tp), unroll=2)
            yp = _ab_out(o, bonus, g, yb, yp, prm, tm_p)
            pw.append(st); psh.append(sh); pcv.append(cv)
            ops, bonus, g, yb, sh, cv = _ab_in(ys, ts, ns_, state_shift[i], state_conv[i], prm)
            o, st = _wkv(ops, bs, ts, state_wkv[i], bb, nb=min(4, bs), tc=ts, unroll=2)
            ys = _ab_out(o, bonus, g, yb, ys, prm, ns_)
            sw.append(st); ssh.append(sh); scv.append(cv)
        else:
            w = fox_in[i]
            wfl = w[:, 3 * d:3 * d + heads_c]
            prm = dict(
                gm=row2(norm_mix[layer]),
                w=jnp.concatenate([w[:, :3 * d], w[:, 3 * d + heads_c:]], axis=1).astype(BF16),
                wfl=jnp.pad(wfl, ((0, 0), (0, LANES - heads_c))).astype(BF16),
                fb=jnp.pad(row2(fox_fb[i]), ((0, 0), (0, LANES - heads_c))),
                qg=row2(jnp.tile(fox_qg[i], heads_c)), kg=row2(jnp.tile(fox_kg[i], heads_c)), bb=bb,
                wflt=wfl.T.astype(BF16), fbt=fox_fb[i].reshape(heads_c, 1).astype(F32),
                tri=(jnp.arange(tm_p)[:, None] <= jnp.arange(tm_p)[None, :]).astype(BF16))
            wo = fox_out[i].astype(BF16)
            q, k, v, og, lf, cum_t = _fox_in(yp, tp, tm_p, prm, True, BF16)
            o = _fox_attn(q, k, v, cum_t, bp, tp, tm_p)
            yp = _fox_out(o, og, yp, wo, tm_p)
            pk.append(k.reshape(bp, tp, heads_c, C_HEAD)); pv.append(v.reshape(bp, tp, heads_c, C_HEAD))
            plf.append(lf.reshape(bp, tp, heads_c))
            q, k, v, og, lf = _fox_in(ys, ts, ns_, prm, False, F32)
            o = _paged_attn(q, k, v, lf, cache_k, cache_v, cache_logf, i, page_table, bs, ts, pps=4)
            ys = _fox_out(o, og, ys, wo, ns_)
            sk.append(k.reshape(bs, ts, heads_c, C_HEAD)); sv.append(v.reshape(bs, ts, heads_c, C_HEAD))
            slf.append(lf.reshape(bs, ts, heads_c))

        wq, wo_x = xq_w[layer].astype(BF16), xo_w[layer].astype(BF16)
        mk, mv = _mem_kv(mem, row2(norm_mem[layer]), xkv_w[layer].astype(BF16), row2(xk_g[layer]), n_mem)
        pmk.append(mk.reshape(bp, n_mem, X_HEADS, d // X_HEADS)); pmv.append(mv.reshape(bp, n_mem, X_HEADS, d // X_HEADS))
        gx, qg = row2(norm_cross[layer]), row2(xq_g[layer])
        yp = _cross(yp, tp, tm_p, mk.reshape(bp, n_mem, d), mv.reshape(bp, n_mem, d), gx, wq, qg, wo_x)
        ys = _cross(ys, ts, ts, cache_mem_k[layer].reshape(bs, n_mem, d), cache_mem_v[layer].reshape(bs, n_mem, d),
                    gx, wq, qg, wo_x)

        gf, wup, cwf, wdn = row2(norm_ffn[layer]), ffn_up[layer].astype(BF16), ffn_conv_w[layer].astype(F32), \
            ffn_down[layer].astype(BF16)
        yp, cvp = _ffn(yp, tp, tm_p, jnp.zeros((bp, CONV_W - 1, d_ff), F32), gf, wup, cwf, wdn)
        ys, cvs = _ffn(ys, ts, ns_, state_ffn_conv[layer], gf, wup, cwf, wdn)
        pfc.append(cvp); sfc.append(cvs)

    st = jnp.stack
    return (yp.reshape(bp, tp, d), ys.reshape(bs, ts, d), st(pw), st(psh), st(pcv), st(pfc), st(pk), st(pv), st(plf),
            st(pmk), st(pmv), st(sw), st(ssh), st(scv), st(sfc), st(sk), st(sv), st(slf))
```

```python
import functools

import jax
import jax.numpy as jnp
from jax import lax
from jax.experimental import pallas as pl
from jax.experimental.pallas import tpu as pltpu

F32 = jnp.float32
BF16 = jnp.bfloat16

RMS_EPS = 1e-6
GN_EPS = 64e-5
L2_EPS = 1e-12
CONV_W = 3
A_HEAD = 64
C_HEAD = 64
X_HEADS = 4
PAGE = 128
LANES = 128
SUBLANES = 8
HALF = 2 * LANES
VMEM_LIMIT = 56 * 1024 * 1024
NEG = -1e30
LOG2E = 1.4426950408889634


def _bdot(a, b):
    return jnp.dot(a.astype(BF16), b.astype(BF16), preferred_element_type=F32)


def _bdot_nt(a, b):
    return lax.dot_general(a.astype(BF16), b.astype(BF16), (((1,), (1,)), ((), ())), preferred_element_type=F32)


def _split3(x):
    x1 = x.astype(BF16)
    r1 = x - x1.astype(F32)
    x2 = r1.astype(BF16)
    x3 = (r1 - x2.astype(F32)).astype(BF16)
    return x1, x2, x3


def _seg_sum(x, bb):
    w = bb.shape[0]
    outs = []
    for c in range(x.shape[1] // w):
        xc = x[:, c * w:(c + 1) * w]
        hi = xc.astype(BF16)
        lo = (xc - hi.astype(F32)).astype(BF16)
        outs.append(jnp.dot(hi, bb, preferred_element_type=F32) + jnp.dot(lo, bb, preferred_element_type=F32))
    return outs[0] if len(outs) == 1 else jnp.concatenate(outs, axis=1)


def _seg_sum1(xb, bb):
    w = bb.shape[0]
    outs = [jnp.dot(xb[:, c * w:(c + 1) * w], bb, preferred_element_type=F32) for c in range(xb.shape[1] // w)]
    return outs[0] if len(outs) == 1 else jnp.concatenate(outs, axis=1)


def _rms_rows(x, g):
    return x * lax.rsqrt(jnp.mean(x * x, axis=-1, keepdims=True) + RMS_EPS) * g


def _sigmoid(x):
    return 1.0 / (1.0 + jnp.exp(-x))


def _softplus(x):
    return jnp.maximum(x, 0.0) + jnp.log(1.0 + jnp.exp(-jnp.abs(x)))


def _pos_in_seq(rows, seg_len):
    if seg_len & (seg_len - 1) == 0:
        return jnp.bitwise_and(rows, seg_len - 1)
    return lax.rem(rows, seg_len)


def _idiv(x, n):
    if n & (n - 1) == 0:
        return jnp.right_shift(x, n.bit_length() - 1)
    return lax.div(x, n)


def _taps(x, rows, seg_len, tail, e0, e1, want2=True):
    r1 = pltpu.roll(x, 1, 0)
    r2 = pltpu.roll(x, 2, 0) if want2 else None
    if tail is not None:
        c1 = tail[SUBLANES - 1:SUBLANES, :]
        xm1 = jnp.where(rows == 0, c1, r1)
        xm2 = None
        if want2:
            c0 = tail[SUBLANES - 2:SUBLANES - 1, :]
            xm2 = jnp.where(rows == 0, c0, jnp.where(rows == 1, c1, r2))
        return xm1, xm2
    pos = _pos_in_seq(rows, seg_len)
    xm1 = jnp.where(pos == 0, e1, r1)
    xm2 = jnp.where(pos == 0, e0, jnp.where(pos == 1, e1, r2)) if want2 else None
    return xm1, xm2


def _cparams(sem, vmem=VMEM_LIMIT):
    return pltpu.CompilerParams(dimension_semantics=sem, vmem_limit_bytes=vmem)


def _const_spec(shape):
    nd = len(shape)
    return pl.BlockSpec(shape, lambda *_: (0,) * nd, pipeline_mode=pl.Buffered(1))


def _ab_in_kernel(carry_mode, seg_len, tiles_per_seq, a_width, b_width, *refs):
    (x_ref, gm_ref, win_ref, mu_ref, w0_ref, w2_ref, a0_ref, a2_ref, g2_ref, kkw_ref, ka_ref, rk_ref,
     cw_ref, bb_ref) = refs[:14]
    if carry_mode:
        init_pa_ref, init_sq_ref = refs[14:16]
        outs = refs[16:27]
        pa_tail, sq_tail = refs[27:29]
    else:
        e_pa_ref, e_sq0_ref, e_sq1_ref = refs[14:17]
        outs = refs[17:28]
    (r_o, w_o, k_o, v_o, kk_o, b_o, bonus_o, g_o, yb_o, pa_o, sq_o) = outs
    tm = x_ref.shape[0]
    a_proj = 3 * a_width + 2 * LANES
    rows = lax.broadcasted_iota(jnp.int32, (tm, 1), 0)

    if carry_mode:
        @pl.when(pl.program_id(0) % tiles_per_seq == 0)
        def _():
            pa_tail[...] = init_pa_ref[0]
            sq_tail[...] = init_sq_ref[0]

    x = x_ref[...]
    h = _rms_rows(x, gm_ref[...]).astype(BF16)
    proj = jnp.dot(h, win_ref[...], preferred_element_type=F32)
    pa = proj[:, :a_proj]
    pb = proj[:, a_proj:]

    if carry_mode:
        prev, _ = _taps(pa, rows, seg_len, pa_tail[...], None, None, want2=False)
    else:
        prev, _ = _taps(pa, rows, seg_len, None, None, e_pa_ref[...], want2=False)
    xs = pa + (prev - pa) * mu_ref[...]
    r = xs[:, :a_width]
    k = xs[:, a_width:2 * a_width]
    v = xs[:, 2 * a_width:3 * a_width]
    wa = xs[:, 3 * a_width:3 * a_width + LANES]
    gd = xs[:, 3 * a_width + LANES:a_proj]
    wl = w0_ref[...] + _bdot(jnp.tanh(wa), w2_ref[...])
    w = -_softplus(-wl) - 0.5
    decay = jnp.exp(-jnp.exp(w))
    a = _sigmoid(a0_ref[...] + _bdot(wa, a2_ref[...]))
    g = _bdot(_sigmoid(gd), g2_ref[...])
    bb = bb_ref[...]
    kk = k * kkw_ref[...]
    kk = kk / jnp.maximum(jnp.sqrt(_seg_sum(kk * kk, bb)), L2_EPS)
    k2 = k * (1.0 + (a - 1.0) * ka_ref[...])
    bonus = _seg_sum(r * k2 * rk_ref[...], bb) * v
    r_o[...] = r
    w_o[...] = decay
    k_o[...] = k2
    v_o[...] = v
    kk_o[...] = kk
    b_o[...] = kk * a
    bonus_o[...] = bonus
    g_o[...] = g

    gb = pb[:, :b_width]
    sq = pb[:, b_width:2 * b_width] * pb[:, 2 * b_width:]
    if carry_mode:
        xm1, xm2 = _taps(sq, rows, seg_len, sq_tail[...], None, None)
    else:
        xm1, xm2 = _taps(sq, rows, seg_len, None, e_sq0_ref[...], e_sq1_ref[...])
    cw = cw_ref[...]
    yb_o[...] = gb * (cw[0:1, :] * xm2 + cw[1:2, :] * xm1 + cw[2:3, :] * sq)

    if carry_mode:
        pa_last = pa[tm - SUBLANES:, :]
        sq_last = sq[tm - SUBLANES:, :]
        pa_tail[...] = pa_last
        sq_tail[...] = sq_last
        pa_o[0] = pa_last
        sq_o[0] = sq_last
    else:
        pa_o[...] = pa
        sq_o[...] = sq


def _tail8(state):
    b, n, c = state.shape
    return jnp.concatenate([jnp.zeros((b, SUBLANES - n, c), state.dtype), state], axis=1)


def _ab_in(x, seg_len, tm, shift_prev, conv_prev, prm):
    n, d = x.shape
    n_seq = n // seg_len
    a_width = prm["w0"].shape[1]
    b_width = prm["cw"].shape[1]
    a_proj = 3 * a_width + 2 * LANES
    carry_mode = seg_len >= tm
    tiles_per_seq = max(seg_len // tm, 1)
    grid = (n // tm,)
    row = lambda c: pl.BlockSpec((tm, c), lambda i: (i, 0))
    params = [prm["gm"], prm["win"], prm["mu"], prm["w0"], prm["w2p"], prm["a0"], prm["a2p"], prm["g2"],
              prm["kkw"], prm["ka"], prm["rk"], prm["cw"], prm["bb"]]
    in_specs = [row(d)] + [_const_spec(p.shape) for p in params]
    out_shape = [jax.ShapeDtypeStruct((n, a_width), F32)] * 8 + [jax.ShapeDtypeStruct((n, b_width), F32)]
    out_specs = [row(a_width)] * 8 + [row(b_width)]
    if carry_mode:
        seq_blk = lambda c: pl.BlockSpec((1, SUBLANES, c), lambda i: (i // tiles_per_seq, 0, 0))
        extra = [_tail8(shift_prev[:, None, :]), _tail8(conv_prev)]
        in_specs += [seq_blk(a_proj), seq_blk(b_width)]
        out_shape += [jax.ShapeDtypeStruct((n_seq, SUBLANES, a_proj), F32),
                      jax.ShapeDtypeStruct((n_seq, SUBLANES, b_width), F32)]
        out_specs += [seq_blk(a_proj), seq_blk(b_width)]
        scratch = [pltpu.VMEM((SUBLANES, a_proj), F32), pltpu.VMEM((SUBLANES, b_width), F32)]
    else:
        extra = [jnp.repeat(shift_prev, seg_len, axis=0), jnp.repeat(conv_prev[:, 0], seg_len, axis=0),
                 jnp.repeat(conv_prev[:, 1], seg_len, axis=0)]
        in_specs += [row(a_proj), row(b_width), row(b_width)]
        out_shape += [jax.ShapeDtypeStruct((n, a_proj), F32), jax.ShapeDtypeStruct((n, b_width), F32)]
        out_specs += [row(a_proj), row(b_width)]
        scratch = []
    outs = pl.pallas_call(
        functools.partial(_ab_in_kernel, carry_mode, seg_len, tiles_per_seq, a_width, b_width),
        grid=grid, in_specs=in_specs, out_specs=out_specs, out_shape=out_shape, scratch_shapes=scratch,
        compiler_params=_cparams(("arbitrary",)),
    )(x, *params, *extra)
    scan_ops = outs[:6]
    bonus, g, yb, pa_t, sq_t = outs[6:]
    if carry_mode:
        shift_new = pa_t[:, SUBLANES - 1]
        conv_new = sq_t[:, SUBLANES - (CONV_W - 1):]
    else:
        shift_new = pa_t.reshape(n_seq, seg_len, a_proj)[:, -1]
        conv_new = sq_t.reshape(n_seq, seg_len, b_width)[:, -(CONV_W - 1):]
    return scan_ops, bonus, g, yb, shift_new, conv_new


def _wkv_kernel(unroll, r_ref, w_ref, k_ref, v_ref, kk_ref, b_ref, s0_ref, bb_ref, eye_ref, hmask_ref,
                o_ref, s_out_ref, s_scr, s_bf, lhs, res):
    nb, tc, width = r_ref.shape
    nh = width // HALF
    heads = width // A_HEAD
    rc = 2 * SUBLANES
    pieces = [(c, h) for c in range(A_HEAD // rc) for h in range(nh)]
    bb = bb_ref[...]

    def rows_of(c):
        return slice(c * rc, (c + 1) * rc)

    def stage_rows(h, kind, c):
        base = (h * 2 + kind) * A_HEAD + c * rc
        return slice(base, base + rc)

    def build(b, t):
        kk, v = kk_ref[b, pl.ds(t, 1), :], v_ref[b, pl.ds(t, 1), :]
        for c, h in pieces:
            ls = slice(h * HALF, (h + 1) * HALF)
            lhs[b, stage_rows(h, 0, c), :] = (s_scr[b, rows_of(c), ls] * kk[:, ls]).astype(BF16)
            lhs[b, stage_rows(h, 1, c), :] = (eye_ref[rows_of(c), :] * v[:, ls]).astype(BF16)
        res[b] = jnp.dot(lhs[b], bb, preferred_element_type=F32)

    def update(b, t):
        row = lambda ref: ref[b, pl.ds(t, 1), :]
        w, bv, k, r = row(w_ref), row(b_ref), row(k_ref), row(r_ref)
        for c, h in pieces:
            ls = slice(h * HALF, (h + 1) * HALF)
            s = (s_scr[b, rows_of(c), ls] * w[:, ls] - res[b, stage_rows(h, 0, c), :] * bv[:, ls]
                 + res[b, stage_rows(h, 1, c), :] * k[:, ls])
            s_scr[b, rows_of(c), ls] = s
            s_bf[b, rows_of(c), ls] = s.astype(BF16)
        rmat = (hmask_ref[...] * r).astype(BF16)
        y = _bdot_nt(rmat, s_bf[b])
        o_ref[b, pl.ds(t, 1)] = y[:heads][None]

    @pl.when(pl.program_id(1) == 0)
    def _():
        s_scr[...] = s0_ref[...]

    for b in range(nb):
        build(b, 0)

    def step(t, carry):
        nxt = jnp.minimum(t + 1, tc - 1)
        for b in range(nb):
            update(b, t)
            build(b, nxt)
        return carry

    lax.fori_loop(0, tc, step, 0, unroll=unroll)

    @pl.when(pl.program_id(1) == pl.num_programs(1) - 1)
    def _():
        s_out_ref[...] = s_scr[...]


def _wkv(scan_ops, n_seq, seg_len, s0, bb, nb, tc, unroll):
    width = scan_ops[0].shape[1]
    heads = width // A_HEAD
    nh = width // HALF
    ops3 = [a.reshape(n_seq, seg_len, width) for a in scan_ops]
    s0_l = jnp.transpose(s0.astype(F32), (0, 2, 1, 3)).reshape(n_seq, A_HEAD, width)
    eye = (jnp.arange(A_HEAD)[:, None] == jnp.arange(HALF)[None, :] % A_HEAD).astype(F32)
    hrows = -(-heads // (2 * SUBLANES)) * 2 * SUBLANES
    hmask = (jnp.arange(hrows)[:, None] == jnp.arange(width)[None, :] // A_HEAD).astype(F32)
    seq_blk = pl.BlockSpec((nb, tc, width), lambda b, c: (b, c, 0))
    st_blk = pl.BlockSpec((nb, A_HEAD, width), lambda b, c: (b, 0, 0))
    stage = (nb, 2 * nh * A_HEAD, HALF)
    o, s_new = pl.pallas_call(
        functools.partial(_wkv_kernel, unroll), grid=(n_seq // nb, seg_len // tc),
        in_specs=[seq_blk] * 6 + [st_blk, _const_spec(bb.shape), _const_spec(eye.shape), _const_spec(hmask.shape)],
        out_specs=[pl.BlockSpec((nb, tc, heads, A_HEAD), lambda b, c: (b, c, 0, 0)), st_blk],
        out_shape=[jax.ShapeDtypeStruct((n_seq, seg_len, heads, A_HEAD), F32),
                   jax.ShapeDtypeStruct((n_seq, A_HEAD, width), F32)],
        scratch_shapes=[pltpu.VMEM((nb, A_HEAD, width), F32), pltpu.VMEM((nb, A_HEAD, width), BF16),
                        pltpu.VMEM(stage, BF16), pltpu.VMEM(stage, F32)],
        compiler_params=_cparams(("arbitrary", "arbitrary")),
    )(*ops3, s0_l, bb, eye, hmask)
    s_new = jnp.transpose(s_new.reshape(n_seq, A_HEAD, heads, A_HEAD), (0, 2, 1, 3))
    return o.reshape(n_seq * seg_len, width), s_new


def _ab_out_kernel(o_ref, bonus_ref, g_ref, yb_ref, x_ref, lnw_ref, lnb_ref, wout_ref, bb_ref, y_ref):
    a_width = o_ref.shape[1]
    o = o_ref[...]
    bb = bb_ref[...]
    mean = _seg_sum(o, bb) * (1.0 / A_HEAD)
    dlt = o - mean
    var = _seg_sum(dlt * dlt, bb) * (1.0 / A_HEAD)
    yn = dlt * lax.rsqrt(var + GN_EPS) * lnw_ref[...] + lnb_ref[...]
    ya = (yn + bonus_ref[...]) * g_ref[...]
    y = _bdot(ya, wout_ref[:a_width, :]) + _bdot(yb_ref[...], wout_ref[a_width:, :])
    y_ref[...] = x_ref[...] + y


def _ab_out(o, bonus, g, yb, x, prm, tm):
    n, d = x.shape
    row = lambda c: pl.BlockSpec((tm, c), lambda i: (i, 0))
    params = [prm["lnw"], prm["lnb"], prm["wout"], prm["bb"]]
    return pl.pallas_call(
        _ab_out_kernel, grid=(n // tm,),
        in_specs=[row(o.shape[1]), row(bonus.shape[1]), row(g.shape[1]), row(yb.shape[1]), row(d)]
        + [_const_spec(p.shape) for p in params],
        out_specs=row(d), out_shape=jax.ShapeDtypeStruct((n, d), F32),
        compiler_params=_cparams(("arbitrary",)),
    )(o, bonus, g, yb, x, *params)


def _mem_kv_kernel(x_ref, gm_ref, wkv_ref, kg_ref, k_ref, v_ref):
    d = x_ref.shape[1]
    hd = d // X_HEADS
    h = _rms_rows(x_ref[...], gm_ref[...])
    kv = _bdot(h, wkv_ref[...])
    kg = kg_ref[...]
    for hh in range(X_HEADS):
        kh = kv[:, hh * hd:(hh + 1) * hd]
        k_ref[:, hh * hd:(hh + 1) * hd] = _rms_rows(kh, kg)
    v_ref[...] = kv[:, d:]


def _mem_kv(mem, gm, wkv, kg, tm):
    n, d = mem.shape
    row = pl.BlockSpec((tm, d), lambda i: (i, 0))
    return pl.pallas_call(
        _mem_kv_kernel, grid=(n // tm,),
        in_specs=[row, _const_spec(gm.shape), _const_spec(wkv.shape), _const_spec(kg.shape)],
        out_specs=[row, row], out_shape=[jax.ShapeDtypeStruct((n, d), F32)] * 2,
        compiler_params=_cparams(("arbitrary",)),
    )(mem, gm, wkv, kg)


def _cross_kernel(x_ref, mk_ref, mv_ref, gm_ref, wq_ref, qg_ref, wo_ref, y_ref, o_scr):
    d = x_ref.shape[1]
    hd = d // X_HEADS
    x = x_ref[...]
    q = _bdot(_rms_rows(x, gm_ref[...]), wq_ref[...])
    qg = qg_ref[...]
    scale = hd ** -0.5
    for hh in range(X_HEADS):
        sl = slice(hh * hd, (hh + 1) * hd)
        qh = _rms_rows(q[:, sl], qg)
        s = _bdot_nt(qh, mk_ref[0, :, sl]) * scale
        p = jnp.exp(s - jnp.max(s, axis=-1, keepdims=True))
        p = p / jnp.sum(p, axis=-1, keepdims=True)
        o_scr[:, sl] = _bdot(p, mv_ref[0, :, sl])
    y_ref[...] = x + _bdot(o_scr[...], wo_ref[...])


def _cross(x, seg_len, tm, mk, mv, gm, wq, qg, wo):
    n, d = x.shape
    tiles_per_seq = seg_len // tm
    row = pl.BlockSpec((tm, d), lambda i: (i, 0))
    mem = pl.BlockSpec((1, mk.shape[1], d), lambda i: (i // tiles_per_seq, 0, 0))
    params = [gm, wq, qg, wo]
    return pl.pallas_call(
        _cross_kernel, grid=(n // tm,),
        in_specs=[row, mem, mem] + [_const_spec(p.shape) for p in params],
        out_specs=row, out_shape=jax.ShapeDtypeStruct((n, d), F32),
        scratch_shapes=[pltpu.VMEM((tm, d), F32)],
        compiler_params=_cparams(("arbitrary",)),
    )(x, mk, mv, *params)


def _ffn_kernel(carry_mode, seg_len, tiles_per_seq, n_chunks, *refs):
    x_ref, gm_ref, wup_ref, cw_ref, wdn_ref = refs[:5]
    if carry_mode:
        init_ref, y_ref, gate_o, tail = refs[5:9]
    else:
        e0_ref, e1_ref, y_ref, gate_o = refs[5:9]
    tm = x_ref.shape[0]
    d_ff = cw_ref.shape[1]
    ck = d_ff // n_chunks
    rows = lax.broadcasted_iota(jnp.int32, (tm, 1), 0)

    if carry_mode:
        @pl.when(pl.program_id(0) % tiles_per_seq == 0)
        def _():
            tail[...] = init_ref[0]

    x = x_ref[...]
    h = _rms_rows(x, gm_ref[...]).astype(BF16)
    acc = x
    for c in range(n_chunks):
        sl = slice(c * ck, (c + 1) * ck)
        gate = jnp.dot(h, wup_ref[:, sl], preferred_element_type=F32)
        up = jnp.dot(h, wup_ref[:, d_ff + c * ck:d_ff + (c + 1) * ck], preferred_element_type=F32)
        if carry_mode:
            xm1, xm2 = _taps(gate, rows, seg_len, tail[:, sl], None, None)
        else:
            xm1, xm2 = _taps(gate, rows, seg_len, None, e0_ref[:, sl], e1_ref[:, sl])
        cv = cw_ref[0:1, sl] * xm2 + cw_ref[1:2, sl] * xm1 + cw_ref[2:3, sl] * gate
        act = cv * _sigmoid(cv) * up
        acc = acc + _bdot(act, wdn_ref[sl, :])
        if carry_mode:
            last = gate[tm - SUBLANES:, :]
            tail[:, sl] = last
            gate_o[0, :, sl] = last
        else:
            gate_o[:, sl] = gate
    y_ref[...] = acc


def _ffn(x, seg_len, tm, conv_prev, gm, wup, cw, wdn, n_chunks=2):
    n, d = x.shape
    d_ff = cw.shape[1]
    n_seq = n // seg_len
    carry_mode = seg_len >= tm
    tiles_per_seq = max(seg_len // tm, 1)
    row = lambda c: pl.BlockSpec((tm, c), lambda i: (i, 0))
    params = [gm, wup, cw, wdn]
    in_specs = [row(d)] + [_const_spec(p.shape) for p in params]
    if carry_mode:
        seq_blk = pl.BlockSpec((1, SUBLANES, d_ff), lambda i: (i // tiles_per_seq, 0, 0))
        extra = [_tail8(conv_prev)]
        in_specs += [seq_blk]
        out_specs = [row(d), seq_blk]
        out_shape = [jax.ShapeDtypeStruct((n, d), F32), jax.ShapeDtypeStruct((n_seq, SUBLANES, d_ff), F32)]
        scratch = [pltpu.VMEM((SUBLANES, d_ff), F32)]
    else:
        extra = [jnp.repeat(conv_prev[:, 0], seg_len, axis=0), jnp.repeat(conv_prev[:, 1], seg_len, axis=0)]
        in_specs += [row(d_ff), row(d_ff)]
        out_specs = [row(d), row(d_ff)]
        out_shape = [jax.ShapeDtypeStruct((n, d), F32), jax.ShapeDtypeStruct((n, d_ff), F32)]
        scratch = []
    y, gate_t = pl.pallas_call(
        functools.partial(_ffn_kernel, carry_mode, seg_len, tiles_per_seq, n_chunks),
        grid=(n // tm,), in_specs=in_specs, out_specs=out_specs, out_shape=out_shape, scratch_shapes=scratch,
        compiler_params=_cparams(("arbitrary",)),
    )(x, *params, *extra)
    if carry_mode:
        conv_new = gate_t[:, SUBLANES - (CONV_W - 1):]
    else:
        conv_new = gate_t.reshape(n_seq, seg_len, d_ff)[:, -(CONV_W - 1):]
    return y, conv_new


def _log_sigmoid(x):
    return jnp.minimum(x, 0.0) - jnp.log(1.0 + jnp.exp(-jnp.abs(x)))


def _fox_in_kernel(with_cum, tiles_per_seq, *refs):
    x_ref, gm_ref, wqg_ref, wkvt_ref, wfl_ref, fb_ref, qg_ref, kgc_ref, bb_ref = refs[:9]
    if with_cum:
        wflt_ref, fbt_ref, tri_ref = refs[9:12]
        q_o, kt_o, vt_o, og_o, lf_o, cum_o, run = refs[12:19]
    else:
        q_o, kt_o, vt_o, og_o, lf_o = refs[9:14]
    tm, d = x_ref.shape
    heads = lf_o.shape[1]
    h = _rms_rows(x_ref[...], gm_ref[...]).astype(BF16)
    proj = jnp.dot(h, wqg_ref[...], preferred_element_type=F32)
    q = proj[:, :d]
    q = q * lax.rsqrt(_seg_sum(q * q, bb_ref[...]) * (1.0 / C_HEAD) + RMS_EPS) * qg_ref[...]
    q_o[...] = (q * (C_HEAD ** -0.5 * LOG2E)).astype(q_o.dtype)
    og_o[...] = proj[:, d:]
    kvt = lax.dot_general(wkvt_ref[...], h, (((1,), (1,)), ((), ())), preferred_element_type=F32)
    k3 = kvt[:d].reshape(heads, C_HEAD, tm)
    ms = jnp.sum(k3 * k3, axis=1, keepdims=True) * (1.0 / C_HEAD)
    gain = jnp.broadcast_to(kgc_ref[...][:, :1], (C_HEAD, tm))
    kt_o[0] = (k3 * lax.rsqrt(ms + RMS_EPS) * gain[None]).reshape(d, tm)
    vt_o[0] = kvt[d:]
    fl = jnp.dot(h, wfl_ref[...], preferred_element_type=F32) + fb_ref[...]
    lf_o[...] = _log_sigmoid(fl)[:, :heads]

    if with_cum:
        @pl.when(pl.program_id(0) % tiles_per_seq == 0)
        def _():
            run[...] = jnp.zeros_like(run)

        flt = lax.dot_general(wflt_ref[...], h, (((1,), (1,)), ((), ())), preferred_element_type=F32)
        lft = _log_sigmoid(flt + fbt_ref[...])
        tri = tri_ref[...]
        cum = run[...][:, :1]
        for part in _split3(lft):
            cum = cum + jnp.dot(part, tri, preferred_element_type=F32)
        cum_o[0] = cum
        run[...] = jnp.broadcast_to(cum[:, cum.shape[1] - 1:], run.shape)


def _fox_in(x, seg_len, tm, prm, with_cum, q_dtype):
    n, d = x.shape
    n_seq = n // seg_len
    heads = d // C_HEAD
    tiles_per_seq = seg_len // tm
    row = lambda c: pl.BlockSpec((tm, c), lambda i: (i, 0))
    tmin = pl.BlockSpec((1, d, tm), lambda i: (i // tiles_per_seq, 0, i % tiles_per_seq))
    params = [prm["gm"], prm["wqg"], prm["wkvt"], prm["wfl"], prm["fb"], prm["qg"], prm["kgc"], prm["bb"]]
    out_shape = [jax.ShapeDtypeStruct((n, d), q_dtype)] + [jax.ShapeDtypeStruct((n_seq, d, seg_len), F32)] * 2 + \
                [jax.ShapeDtypeStruct((n, d), F32), jax.ShapeDtypeStruct((n, heads), F32)]
    out_specs = [row(d), tmin, tmin, row(d), row(heads)]
    scratch = []
    if with_cum:
        params += [prm["wflt"], prm["fbt"], prm["tri"]]
        out_shape += [jax.ShapeDtypeStruct((n_seq, heads, seg_len), F32)]
        out_specs += [pl.BlockSpec((1, heads, tm), lambda i: (i // tiles_per_seq, 0, i % tiles_per_seq))]
        scratch = [pltpu.VMEM((heads, LANES), F32)]
    return pl.pallas_call(
        functools.partial(_fox_in_kernel, with_cum, tiles_per_seq),
        grid=(n // tm,), in_specs=[row(d)] + [_const_spec(p.shape) for p in params],
        out_specs=out_specs, out_shape=out_shape, scratch_shapes=scratch,
        compiler_params=_cparams(("arbitrary",)),
    )(x, *params)


def _fox_attn_kernel(tk, q_ref, k_ref, v_ref, cum_ref, o_ref, acc, mrun, s_a, s_b):
    tq = q_ref.shape[0]
    pair = pl.program_id(1)
    qi = pl.program_id(2)
    lane = lax.broadcasted_iota(jnp.int32, (1, LANES), 1)
    first = lane < C_HEAD
    q = q_ref[...]
    zero = jnp.zeros_like(q)
    qs = (jnp.where(first, q, zero), jnp.where(first, zero, q))
    acc[...] = jnp.zeros_like(acc)
    mrun[...] = jnp.full_like(mrun, NEG)

    top = lax.broadcasted_iota(jnp.int32, (LANES, 1), 0) < C_HEAD

    def cols(j):
        return pl.ds(pl.multiple_of(j * tk, tk), tk)

    def qk(j, s_ref):
        kb = k_ref[0, :, cols(j)].astype(BF16)
        for hh in range(2):
            s_ref[hh] = jnp.dot(qs[hh], kb, preferred_element_type=F32)

    def softmax_pv(j, s_ref, masked):
        vb = v_ref[0, :, cols(j)].astype(BF16)
        one = jnp.ones_like(vb)
        vs = (jnp.where(top, vb, one), jnp.where(top, one, vb))
        for hh in range(2):
            bias = cum_ref[0, 2 * pair + hh, pl.ds(j, 1), :] * (-LOG2E)
            s = s_ref[hh] + bias
            if masked:
                rr = lax.broadcasted_iota(jnp.int32, (tq, tk), 0)
                cc = lax.broadcasted_iota(jnp.int32, (tq, tk), 1)
                s = jnp.where(cc <= rr, s, NEG)
            m_old = mrun[hh]
            m_new = jnp.maximum(m_old, jnp.max(s, axis=-1, keepdims=True))
            p = jnp.exp2(s - m_new[:, :1])
            acc[hh] = jnp.exp2(m_old - m_new) * acc[hh] + _bdot_nt(p, vs[hh])
            mrun[hh] = m_new

    pairs = qi // 2
    qk(0, s_a)

    def body(jj, carry):
        qk(2 * jj + 1, s_b)
        softmax_pv(2 * jj, s_a, False)
        qk(2 * jj + 2, s_a)
        softmax_pv(2 * jj + 1, s_b, False)
        return carry

    lax.fori_loop(0, pairs, body, 0)

    @pl.when(qi == 2 * pairs)
    def _():
        softmax_pv(qi, s_a, True)

    @pl.when(qi != 2 * pairs)
    def _():
        qk(qi, s_b)
        softmax_pv(qi - 1, s_a, False)
        softmax_pv(qi, s_b, True)

    a0 = acc[0]
    a1 = acc[1]
    o_ref[...] = jnp.where(first, a0 / pltpu.roll(a0, C_HEAD, 1), a1 / pltpu.roll(a1, C_HEAD, 1))


def _fox_attn(q, kt, vt, cum_t, n_seq, seg_len, tq):
    n, d = q.shape
    heads = d // C_HEAD
    nq = seg_len // tq
    cum4 = cum_t.reshape(n_seq, heads, nq, tq)
    qblk = pl.BlockSpec((tq, LANES), lambda b, p, i: (b * nq + i, p))
    kblk = pl.BlockSpec((1, LANES, seg_len), lambda b, p, i: (b, p, 0))
    cblk = pl.BlockSpec((1, heads, nq, tq), lambda b, p, i: (b, 0, 0, 0))
    return pl.pallas_call(
        functools.partial(_fox_attn_kernel, tq),
        grid=(n_seq, d // LANES, nq),
        in_specs=[qblk, kblk, kblk, cblk],
        out_specs=qblk, out_shape=jax.ShapeDtypeStruct((n, d), F32),
        scratch_shapes=[pltpu.VMEM((2, tq, LANES), F32), pltpu.VMEM((2, tq, LANES), F32),
                        pltpu.VMEM((2, tq, tq), F32), pltpu.VMEM((2, tq, tq), F32)],
        compiler_params=_cparams(("arbitrary", "arbitrary", "arbitrary")),
    )(q, kt, vt, cum4)


def _fox_out_kernel(o_ref, og_ref, x_ref, w_ref, y_ref):
    y_ref[...] = x_ref[...] + _bdot(o_ref[...] * _sigmoid(og_ref[...]), w_ref[...])


def _fox_out(o, og, x, w, tm):
    n, d = x.shape
    row = pl.BlockSpec((tm, d), lambda i: (i, 0))
    return pl.pallas_call(
        _fox_out_kernel, grid=(n // tm,), in_specs=[row, row, row, _const_spec(w.shape)],
        out_specs=row, out_shape=jax.ShapeDtypeStruct((n, d), F32),
        compiler_params=_cparams(("arbitrary",)),
    )(o, og, x, w)


def _paged_kernel(pps, n_new, *refs):
    pt_ref = refs[0]
    q_ref = refs[1]
    k_refs = refs[2:2 + pps]
    v_refs = refs[2 + pps:2 + 2 * pps]
    lf_refs = refs[2 + 2 * pps:2 + 3 * pps]
    kn_ref, vn_ref, lfn_ref, hsel_ref, tri_ref, o_ref = refs[2 + 3 * pps:8 + 3 * pps]
    qexp, acc, mrun, lrun, crun, kt_all, vt_all, lf_all = refs[8 + 3 * pps:]
    del pt_ref
    step = pl.program_id(1)
    last = pl.num_programs(1) - 1
    rows, d = qexp.shape
    heads = rows // n_new
    rhead = _idiv(lax.broadcasted_iota(jnp.int32, (rows, 1), 0), n_new)
    lhead = _idiv(lax.broadcasted_iota(jnp.int32, (1, d), 1), C_HEAD)

    @pl.when(step == 0)
    def _():
        qf = q_ref[...].astype(F32)
        qt = jnp.concatenate([qf] * heads, axis=0)
        qexp[...] = jnp.where(rhead == lhead, qt, 0.0).astype(BF16)
        acc[...] = jnp.zeros_like(acc)
        mrun[...] = jnp.full_like(mrun, NEG)
        lrun[...] = jnp.zeros_like(lrun)
        crun[...] = jnp.zeros_like(crun)

    def block(kt, vt, lft, npg, new_tokens):
        s = jnp.dot(qexp[...], kt, preferred_element_type=F32)
        parts = [jnp.dot(hsel_ref[...], part, preferred_element_type=F32).astype(BF16) for part in _split3(lft)]
        stacked = jnp.concatenate([part[:, i * PAGE:(i + 1) * PAGE] for part in parts for i in range(npg)], axis=0)
        loc = jnp.dot(stacked, tri_ref[...], preferred_element_type=F32)
        off = crun[...]
        cols = []
        for i in range(npg):
            cum = off
            for pi in range(len(parts)):
                cum = cum + loc[(pi * npg + i) * rows:(pi * npg + i + 1) * rows, :]
            cols.append(s[:, i * PAGE:(i + 1) * PAGE] - cum * LOG2E)
            off = jnp.broadcast_to(cum[:, PAGE - 1:], (rows, PAGE))
        crun[...] = off
        s = cols[0] if npg == 1 else jnp.concatenate(cols, axis=1)
        if new_tokens:
            qpos = _pos_in_seq(lax.broadcasted_iota(jnp.int32, (rows, PAGE), 0), n_new)
            kpos = lax.broadcasted_iota(jnp.int32, (rows, PAGE), 1)
            s = jnp.where(kpos <= qpos, s, NEG)
        m_old = mrun[...][:, :1]
        m_new = jnp.maximum(m_old, jnp.max(s, axis=-1, keepdims=True))
        alpha = jnp.exp2(m_old - m_new)
        p = jnp.exp2(s - m_new)
        lrun[...] = alpha * lrun[...] + jnp.sum(p, axis=-1, keepdims=True)
        acc[...] = alpha * acc[...] + _bdot_nt(p, vt)
        mrun[...] = jnp.broadcast_to(m_new, mrun.shape)

    @pl.when(step < last)
    def _():
        for i in range(pps):
            kt_all[:, i * PAGE:(i + 1) * PAGE] = k_refs[i][0, 0].astype(BF16)
            vt_all[:, i * PAGE:(i + 1) * PAGE] = v_refs[i][0, 0].astype(BF16)
            lf_all[:, i * PAGE:(i + 1) * PAGE] = lf_refs[i][0, 0]
        block(kt_all[...], vt_all[...], lf_all[...], pps, False)

    @pl.when(step == last)
    def _():
        block(kn_ref[0].astype(BF16), vn_ref[0].astype(BF16), lfn_ref[0], 1, True)
        res = acc[...] / lrun[...][:, :1]
        out = jnp.zeros((n_new, d), F32)
        for hh in range(heads):
            out = out + jnp.where(lhead == hh, res[hh * n_new:(hh + 1) * n_new, :], 0.0)
        o_ref[...] = out


def _paged_attn(q, kt_new, vt_new, lf_new, cache_k, cache_v, cache_lf, layer, page_table, n_seq, n_new, pps):
    n, d = q.shape
    heads = d // C_HEAD
    rows = heads * n_new
    n_pages = page_table.shape[1]
    steps = n_pages // pps
    nc, n_pool = cache_k.shape[:2]
    ck = jnp.transpose(cache_k, (0, 1, 3, 4, 2)).reshape(nc, n_pool, d, PAGE)
    cv = jnp.transpose(cache_v, (0, 1, 3, 4, 2)).reshape(nc, n_pool, d, PAGE)
    cache_lf = jnp.transpose(cache_lf, (0, 1, 3, 2))
    pad = lambda at: jnp.pad(jnp.transpose(at.reshape(at.shape[0], n_seq, n_new), (1, 0, 2)),
                             ((0, 0), (0, 0), (0, PAGE - n_new)))
    hsel = (jnp.arange(rows)[:, None] // n_new == jnp.arange(heads)[None, :]).astype(BF16)
    tri = (jnp.arange(PAGE)[:, None] <= jnp.arange(PAGE)[None, :]).astype(BF16)

    def page_spec(i, width):
        return pl.BlockSpec((1, 1, width, PAGE),
                            lambda b, s, pt: (layer, pt[b, jnp.minimum(s, steps - 1) * pps + i], 0, 0))

    new_spec = lambda width: pl.BlockSpec((1, width, PAGE), lambda b, s, pt: (b, 0, 0))
    const = lambda shape: pl.BlockSpec(shape, lambda b, s, pt: (0, 0))
    qspec = pl.BlockSpec((n_new, d), lambda b, s, pt: (b, 0))
    grid_spec = pltpu.PrefetchScalarGridSpec(
        num_scalar_prefetch=1, grid=(n_seq, steps + 1),
        in_specs=[qspec] + [page_spec(i, d) for i in range(pps)] * 2 + [page_spec(i, heads) for i in range(pps)]
        + [new_spec(d), new_spec(d), new_spec(heads), const(hsel.shape), const(tri.shape)],
        out_specs=qspec,
        scratch_shapes=[pltpu.VMEM((rows, d), BF16), pltpu.VMEM((rows, d), F32), pltpu.VMEM((rows, PAGE), F32),
                        pltpu.VMEM((rows, PAGE), F32), pltpu.VMEM((rows, PAGE), F32),
                        pltpu.VMEM((d, pps * PAGE), BF16), pltpu.VMEM((d, pps * PAGE), BF16),
                        pltpu.VMEM((heads, pps * PAGE), F32)])
    return pl.pallas_call(
        functools.partial(_paged_kernel, pps, n_new), grid_spec=grid_spec,
        out_shape=jax.ShapeDtypeStruct((n, d), F32),
        compiler_params=_cparams(("arbitrary", "arbitrary")),
    )(page_table, q, *([ck] * pps), *([cv] * pps), *([cache_lf] * pps), pad(kt_new), pad(vt_new), pad(lf_new.T),
      hsel, tri)


def _block_ones(width, block):
    i = jnp.arange(width)
    return (i[:, None] // block == i[None, :] // block).astype(BF16)


def kernel(x_prompt, x_sample, mem_prompt, state_wkv, state_shift, state_conv, state_ffn_conv, cache_k, cache_v, cache_logf, cache_mem_k, cache_mem_v, page_table, norm_mix, norm_cross, norm_mem, norm_ffn, ab_in, ab_out, rw_mu, rw_w0, rw_w2, rw_a0, rw_a2, rw_g2, rw_kk, rw_ka, rw_rk, rw_lnw, rw_lnb, sc_conv_w, fox_in, fox_out, fox_qg, fox_kg, fox_fb, xq_w, xkv_w, xo_w, xq_g, xk_g, ffn_up, ffn_conv_w, ffn_down):
    bp, tp, d = x_prompt.shape
    bs, ts, _ = x_sample.shape
    n_mem = mem_prompt.shape[1]
    depth = norm_mix.shape[0]
    heads_c = d // C_HEAD
    a_width = rw_w0.shape[1]
    b_width = sc_conv_w.shape[2]
    d_ff = ffn_conv_w.shape[2]
    np_, ns_ = bp * tp, bs * ts
    tm_p = min(512, tp)
    row2 = lambda a: a.reshape(1, -1).astype(F32)
    bb = _block_ones(HALF, A_HEAD)

    yp = x_prompt.reshape(np_, d)
    ys = x_sample.reshape(ns_, d)
    mem = mem_prompt.reshape(bp * n_mem, d)
    pw, psh, pcv, pfc, pk, pv, plf, pmk, pmv = [], [], [], [], [], [], [], [], []
    sw, ssh, scv, sfc, sk, sv, slf = [], [], [], [], [], [], []

    for layer in range(depth):
        i = layer // 2
        if layer % 2 == 0:
            zrow = jnp.zeros((LANES // 2, a_width), F32)
            prm = dict(
                gm=row2(norm_mix[layer]), win=ab_in[i].astype(BF16), mu=row2(rw_mu[i]), w0=row2(rw_w0[i]),
                w2p=jnp.concatenate([rw_w2[i], zrow], axis=0).astype(BF16), a0=row2(rw_a0[i]),
                a2p=jnp.concatenate([zrow, rw_a2[i]], axis=0).astype(BF16), g2=rw_g2[i].astype(BF16),
                kkw=row2(rw_kk[i]), ka=row2(rw_ka[i]), rk=row2(rw_rk[i]), cw=sc_conv_w[i].astype(F32), bb=bb,
                lnw=row2(rw_lnw[i]), lnb=row2(rw_lnb[i]), wout=ab_out[i].astype(BF16))
            a_proj = ab_in.shape[2] - 3 * b_width
            ops, bonus, g, yb, sh, cv = _ab_in(yp, tp, tm_p, jnp.zeros((bp, a_proj), F32),
                                               jnp.zeros((bp, CONV_W - 1, b_width), F32), prm)
            o, st = _wkv(ops, bp, tp, jnp.zeros((bp, a_width // A_HEAD, A_HEAD, A_HEAD), F32), bb,
                         nb=min(4, bp), tc=min(128, tp), unroll=2)
            yp = _ab_out(o, bonus, g, yb, yp, prm, tm_p)
            pw.append(st); psh.append(sh); pcv.append(cv)
            ops, bonus, g, yb, sh, cv = _ab_in(ys, ts, ns_, state_shift[i], state_conv[i], prm)
            o, st = _wkv(ops, bs, ts, state_wkv[i], bb, nb=min(4, bs), tc=ts, unroll=2)
            ys = _ab_out(o, bonus, g, yb, ys, prm, ns_)
            sw.append(st); ssh.append(sh); scv.append(cv)
        else:
            w = fox_in[i]
            wfl = w[:, 3 * d:3 * d + heads_c]
            prm = dict(
                gm=row2(norm_mix[layer]),
                wqg=jnp.concatenate([w[:, :d], w[:, 3 * d + heads_c:]], axis=1).astype(BF16),
                wkvt=w[:, d:3 * d].T.astype(BF16),
                wfl=jnp.pad(wfl, ((0, 0), (0, LANES - heads_c))).astype(BF16),
                fb=jnp.pad(row2(fox_fb[i]), ((0, 0), (0, LANES - heads_c))),
                qg=row2(jnp.tile(fox_qg[i], heads_c)),
                kgc=jnp.broadcast_to(fox_kg[i].astype(F32)[:, None], (C_HEAD, LANES)), bb=bb,
                wflt=wfl.T.astype(BF16), fbt=fox_fb[i].reshape(heads_c, 1).astype(F32),
                tri=(jnp.arange(tm_p)[:, None] <= jnp.arange(tm_p)[None, :]).astype(BF16))
            wo = fox_out[i].astype(BF16)
            tok_major = lambda zt, b, t: jnp.transpose(zt.reshape(b, heads_c, C_HEAD, t), (0, 3, 1, 2))
            q, kt, vt, og, lf, cum_t = _fox_in(yp, tp, tm_p, prm, True, BF16)
            o = _fox_attn(q, kt, vt, cum_t, bp, tp, tm_p)
            yp = _fox_out(o, og, yp, wo, tm_p)
            pk.append(tok_major(kt, bp, tp)); pv.append(tok_major(vt, bp, tp))
            plf.append(lf.reshape(bp, tp, heads_c))
            q, kt, vt, og, lf = _fox_in(ys, ns_, ns_, prm, False, F32)
            o = _paged_attn(q, kt[0], vt[0], lf, cache_k, cache_v, cache_logf, i, page_table, bs, ts, pps=8)
            ys = _fox_out(o, og, ys, wo, ns_)
            sk.append(tok_major(kt, 1, ns_).reshape(bs, ts, heads_c, C_HEAD))
            sv.append(tok_major(vt, 1, ns_).reshape(bs, ts, heads_c, C_HEAD))
            slf.append(lf.reshape(bs, ts, heads_c))

        wq, wo_x = xq_w[layer].astype(BF16), xo_w[layer].astype(BF16)
        mk, mv = _mem_kv(mem, row2(norm_mem[layer]), xkv_w[layer].astype(BF16), row2(xk_g[layer]), n_mem)
        pmk.append(mk.reshape(bp, n_mem, X_HEADS, d // X_HEADS)); pmv.append(mv.reshape(bp, n_mem, X_HEADS, d // X_HEADS))
        gx, qg = row2(norm_cross[layer]), row2(xq_g[layer])
        yp = _cross(yp, tp, tm_p, mk.reshape(bp, n_mem, d), mv.reshape(bp, n_mem, d), gx, wq, qg, wo_x)
        ys = _cross(ys, ts, ts, cache_mem_k[layer].reshape(bs, n_mem, d), cache_mem_v[layer].reshape(bs, n_mem, d),
                    gx, wq, qg, wo_x)

        gf, wup, cwf, wdn = row2(norm_ffn[layer]), ffn_up[layer].astype(BF16), ffn_conv_w[layer].astype(F32), \
            ffn_down[layer].astype(BF16)
        yp, cvp = _ffn(yp, tp, tm_p, jnp.zeros((bp, CONV_W - 1, d_ff), F32), gf, wup, cwf, wdn)
        ys, cvs = _ffn(ys, ts, ns_, state_ffn_conv[layer], gf, wup, cwf, wdn)
        pfc.append(cvp); sfc.append(cvs)

    st = jnp.stack
    return (yp.reshape(bp, tp, d), ys.reshape(bs, ts, d), st(pw), st(psh), st(pcv), st(pfc), st(pk), st(pv), st(plf),
            st(pmk), st(pmv), st(sw), st(ssh), st(scv), st(sfc), st(sk), st(sv), st(slf))
```

```python
import functools

import jax
import jax.numpy as jnp
from jax import lax
from jax.experimental import pallas as pl
from jax.experimental.pallas import tpu as pltpu

F32 = jnp.float32
BF16 = jnp.bfloat16

RMS_EPS = 1e-6
GN_EPS = 64e-5
L2_EPS = 1e-12
CONV_W = 3
A_HEAD = 64
C_HEAD = 64
X_HEADS = 4
PAGE = 128
LANES = 128
SUBLANES = 8
HALF = 2 * LANES
VMEM_LIMIT = 56 * 1024 * 1024
NEG = -1e30
LOG2E = 1.4426950408889634


def _bdot(a, b):
    return jnp.dot(a.astype(BF16), b.astype(BF16), preferred_element_type=F32)


def _bdot_nt(a, b):
    return lax.dot_general(a.astype(BF16), b.astype(BF16), (((1,), (1,)), ((), ())), preferred_element_type=F32)


def _split3(x):
    x1 = x.astype(BF16)
    r1 = x - x1.astype(F32)
    x2 = r1.astype(BF16)
    x3 = (r1 - x2.astype(F32)).astype(BF16)
    return x1, x2, x3


def _seg_sum(x, bb):
    w = bb.shape[0]
    outs = []
    for c in range(x.shape[1] // w):
        xc = x[:, c * w:(c + 1) * w]
        hi = xc.astype(BF16)
        lo = (xc - hi.astype(F32)).astype(BF16)
        outs.append(jnp.dot(hi, bb, preferred_element_type=F32) + jnp.dot(lo, bb, preferred_element_type=F32))
    return outs[0] if len(outs) == 1 else jnp.concatenate(outs, axis=1)


def _seg_sum1(xb, bb):
    w = bb.shape[0]
    outs = [jnp.dot(xb[:, c * w:(c + 1) * w], bb, preferred_element_type=F32) for c in range(xb.shape[1] // w)]
    return outs[0] if len(outs) == 1 else jnp.concatenate(outs, axis=1)


def _rms_rows(x, g):
    return x * lax.rsqrt(jnp.mean(x * x, axis=-1, keepdims=True) + RMS_EPS) * g


def _sigmoid(x):
    return 1.0 / (1.0 + jnp.exp(-x))


def _softplus(x):
    return jnp.maximum(x, 0.0) + jnp.log(1.0 + jnp.exp(-jnp.abs(x)))


def _pos_in_seq(rows, seg_len):
    if seg_len & (seg_len - 1) == 0:
        return jnp.bitwise_and(rows, seg_len - 1)
    return lax.rem(rows, seg_len)


def _idiv(x, n):
    if n & (n - 1) == 0:
        return jnp.right_shift(x, n.bit_length() - 1)
    return lax.div(x, n)


def _taps(x, rows, seg_len, tail, e0, e1, want2=True):
    r1 = pltpu.roll(x, 1, 0)
    r2 = pltpu.roll(x, 2, 0) if want2 else None
    if tail is not None:
        c1 = tail[SUBLANES - 1:SUBLANES, :]
        xm1 = jnp.where(rows == 0, c1, r1)
        xm2 = None
        if want2:
            c0 = tail[SUBLANES - 2:SUBLANES - 1, :]
            xm2 = jnp.where(rows == 0, c0, jnp.where(rows == 1, c1, r2))
        return xm1, xm2
    pos = _pos_in_seq(rows, seg_len)
    xm1 = jnp.where(pos == 0, e1, r1)
    xm2 = jnp.where(pos == 0, e0, jnp.where(pos == 1, e1, r2)) if want2 else None
    return xm1, xm2


def _cparams(sem, vmem=VMEM_LIMIT):
    return pltpu.CompilerParams(dimension_semantics=sem, vmem_limit_bytes=vmem)


def _const_spec(shape):
    nd = len(shape)
    return pl.BlockSpec(shape, lambda *_: (0,) * nd, pipeline_mode=pl.Buffered(1))


def _ab_in_kernel(carry_mode, seg_len, tiles_per_seq, a_width, b_width, *refs):
    (x_ref, gm_ref, win_ref, mu_ref, w0_ref, w2_ref, a0_ref, a2_ref, g2_ref, kkw_ref, ka_ref, rk_ref,
     cw_ref, bb_ref) = refs[:14]
    if carry_mode:
        init_pa_ref, init_sq_ref = refs[14:16]
        outs = refs[16:27]
        pa_tail, sq_tail = refs[27:29]
    else:
        e_pa_ref, e_sq0_ref, e_sq1_ref = refs[14:17]
        outs = refs[17:28]
    (r_o, w_o, k_o, v_o, kk_o, b_o, bonus_o, g_o, yb_o, pa_o, sq_o) = outs
    tm = x_ref.shape[0]
    a_proj = 3 * a_width + 2 * LANES
    rows = lax.broadcasted_iota(jnp.int32, (tm, 1), 0)

    if carry_mode:
        @pl.when(pl.program_id(0) % tiles_per_seq == 0)
        def _():
            pa_tail[...] = init_pa_ref[0]
            sq_tail[...] = init_sq_ref[0]

    x = x_ref[...]
    h = _rms_rows(x, gm_ref[...]).astype(BF16)
    proj = jnp.dot(h, win_ref[...], preferred_element_type=F32)
    pa = proj[:, :a_proj]
    pb = proj[:, a_proj:]

    if carry_mode:
        prev, _ = _taps(pa, rows, seg_len, pa_tail[...], None, None, want2=False)
    else:
        prev, _ = _taps(pa, rows, seg_len, None, None, e_pa_ref[...], want2=False)
    xs = pa + (prev - pa) * mu_ref[...]
    r = xs[:, :a_width]
    k = xs[:, a_width:2 * a_width]
    v = xs[:, 2 * a_width:3 * a_width]
    wa = xs[:, 3 * a_width:3 * a_width + LANES]
    gd = xs[:, 3 * a_width + LANES:a_proj]
    wl = w0_ref[...] + _bdot(jnp.tanh(wa), w2_ref[...])
    w = -_softplus(-wl) - 0.5
    decay = jnp.exp(-jnp.exp(w))
    a = _sigmoid(a0_ref[...] + _bdot(wa, a2_ref[...]))
    g = _bdot(_sigmoid(gd), g2_ref[...])
    bb = bb_ref[...]
    kk = k * kkw_ref[...]
    kk = kk / jnp.maximum(jnp.sqrt(_seg_sum(kk * kk, bb)), L2_EPS)
    k2 = k * (1.0 + (a - 1.0) * ka_ref[...])
    bonus = _seg_sum(r * k2 * rk_ref[...], bb) * v
    r_o[...] = r
    w_o[...] = decay
    k_o[...] = k2
    v_o[...] = v
    kk_o[...] = kk
    b_o[...] = kk * a
    bonus_o[...] = bonus
    g_o[...] = g

    gb = pb[:, :b_width]
    sq = pb[:, b_width:2 * b_width] * pb[:, 2 * b_width:]
    if carry_mode:
        xm1, xm2 = _taps(sq, rows, seg_len, sq_tail[...], None, None)
    else:
        xm1, xm2 = _taps(sq, rows, seg_len, None, e_sq0_ref[...], e_sq1_ref[...])
    cw = cw_ref[...]
    yb_o[...] = gb * (cw[0:1, :] * xm2 + cw[1:2, :] * xm1 + cw[2:3, :] * sq)

    if carry_mode:
        pa_last = pa[tm - SUBLANES:, :]
        sq_last = sq[tm - SUBLANES:, :]
        pa_tail[...] = pa_last
        sq_tail[...] = sq_last
        pa_o[0] = pa_last
        sq_o[0] = sq_last
    else:
        pa_o[...] = pa
        sq_o[...] = sq


def _tail8(state):
    b, n, c = state.shape
    return jnp.concatenate([jnp.zeros((b, SUBLANES - n, c), state.dtype), state], axis=1)


def _ab_in(x, seg_len, tm, shift_prev, conv_prev, prm):
    n, d = x.shape
    n_seq = n // seg_len
    a_width = prm["w0"].shape[1]
    b_width = prm["cw"].shape[1]
    a_proj = 3 * a_width + 2 * LANES
    carry_mode = seg_len >= tm
    tiles_per_seq = max(seg_len // tm, 1)
    grid = (n // tm,)
    row = lambda c: pl.BlockSpec((tm, c), lambda i: (i, 0))
    params = [prm["gm"], prm["win"], prm["mu"], prm["w0"], prm["w2p"], prm["a0"], prm["a2p"], prm["g2"],
              prm["kkw"], prm["ka"], prm["rk"], prm["cw"], prm["bb"]]
    in_specs = [row(d)] + [_const_spec(p.shape) for p in params]
    out_shape = [jax.ShapeDtypeStruct((n, a_width), F32)] * 8 + [jax.ShapeDtypeStruct((n, b_width), F32)]
    out_specs = [row(a_width)] * 8 + [row(b_width)]
    if carry_mode:
        seq_blk = lambda c: pl.BlockSpec((1, SUBLANES, c), lambda i: (i // tiles_per_seq, 0, 0))
        extra = [_tail8(shift_prev[:, None, :]), _tail8(conv_prev)]
        in_specs += [seq_blk(a_proj), seq_blk(b_width)]
        out_shape += [jax.ShapeDtypeStruct((n_seq, SUBLANES, a_proj), F32),
                      jax.ShapeDtypeStruct((n_seq, SUBLANES, b_width), F32)]
        out_specs += [seq_blk(a_proj), seq_blk(b_width)]
        scratch = [pltpu.VMEM((SUBLANES, a_proj), F32), pltpu.VMEM((SUBLANES, b_width), F32)]
    else:
        extra = [jnp.repeat(shift_prev, seg_len, axis=0), jnp.repeat(conv_prev[:, 0], seg_len, axis=0),
                 jnp.repeat(conv_prev[:, 1], seg_len, axis=0)]
        in_specs += [row(a_proj), row(b_width), row(b_width)]
        out_shape += [jax.ShapeDtypeStruct((n, a_proj), F32), jax.ShapeDtypeStruct((n, b_width), F32)]
        out_specs += [row(a_proj), row(b_width)]
        scratch = []
    outs = pl.pallas_call(
        functools.partial(_ab_in_kernel, carry_mode, seg_len, tiles_per_seq, a_width, b_width),
        grid=grid, in_specs=in_specs, out_specs=out_specs, out_shape=out_shape, scratch_shapes=scratch,
        compiler_params=_cparams(("arbitrary",)),
    )(x, *params, *extra)
    scan_ops = outs[:6]
    bonus, g, yb, pa_t, sq_t = outs[6:]
    if carry_mode:
        shift_new = pa_t[:, SUBLANES - 1]
        conv_new = sq_t[:, SUBLANES - (CONV_W - 1):]
    else:
        shift_new = pa_t.reshape(n_seq, seg_len, a_proj)[:, -1]
        conv_new = sq_t.reshape(n_seq, seg_len, b_width)[:, -(CONV_W - 1):]
    return scan_ops, bonus, g, yb, shift_new, conv_new


def _wkv_kernel(unroll, r_ref, w_ref, k_ref, v_ref, kk_ref, b_ref, s0_ref, bb_ref, eye_ref, hmask_ref,
                o_ref, s_out_ref, s_scr, s_bf, lhs, res):
    nb, tc, width = r_ref.shape
    nh = width // HALF
    heads = width // A_HEAD
    rc = 2 * SUBLANES
    chunks = range(A_HEAD // rc)
    bb = bb_ref[...]

    def rows_of(c):
        return slice(c * rc, (c + 1) * rc)

    def stage_rows(kind, c):
        return slice(kind * A_HEAD + c * rc, kind * A_HEAD + (c + 1) * rc)

    def build(b, h, t, slot):
        ls = slice(h * HALF, (h + 1) * HALF)
        kk, v = kk_ref[b, pl.ds(t, 1), ls], v_ref[b, pl.ds(t, 1), ls]
        for c in chunks:
            lhs[slot, b * nh + h, stage_rows(0, c), :] = (s_scr[b, rows_of(c), ls] * kk).astype(BF16)
            lhs[slot, b * nh + h, stage_rows(1, c), :] = (eye_ref[rows_of(c), :] * v).astype(BF16)
        res[slot, b * nh + h] = jnp.dot(lhs[slot, b * nh + h], bb, preferred_element_type=F32)

    def update(b, h, t, slot):
        ls = slice(h * HALF, (h + 1) * HALF)
        row = lambda ref: ref[b, pl.ds(t, 1), ls]
        w, bv, k = row(w_ref), row(b_ref), row(k_ref)
        for c in chunks:
            s = (s_scr[b, rows_of(c), ls] * w - res[slot, b * nh + h, stage_rows(0, c), :] * bv
                 + res[slot, b * nh + h, stage_rows(1, c), :] * k)
            s_scr[b, rows_of(c), ls] = s
            s_bf[b, rows_of(c), ls] = s.astype(BF16)

    def readout(b, t):
        rmat = (hmask_ref[...] * r_ref[b, pl.ds(t, 1), :]).astype(BF16)
        y = _bdot_nt(rmat, s_bf[b])
        o_ref[b, pl.ds(t, 1)] = y[:heads][None]

    @pl.when(pl.program_id(1) == 0)
    def _():
        s_scr[...] = s0_ref[...]

    for b in range(nb):
        for h in range(nh):
            build(b, h, 0, unroll - 1)

    def step(i, carry):
        for u in range(unroll):
            t = i * unroll + u
            nxt = jnp.minimum(t + 1, tc - 1)
            for b in range(nb):
                for h in range(nh):
                    update(b, h, t, (u - 1) % unroll)
                    build(b, h, nxt, u)
                readout(b, t)
        return carry

    lax.fori_loop(0, tc // unroll, step, 0)

    @pl.when(pl.program_id(1) == pl.num_programs(1) - 1)
    def _():
        s_out_ref[...] = s_scr[...]


def _wkv(scan_ops, n_seq, seg_len, s0, bb, nb, tc, unroll):
    width = scan_ops[0].shape[1]
    heads = width // A_HEAD
    nh = width // HALF
    ops3 = [a.reshape(n_seq, seg_len, width) for a in scan_ops]
    s0_l = jnp.transpose(s0.astype(F32), (0, 2, 1, 3)).reshape(n_seq, A_HEAD, width)
    eye = (jnp.arange(A_HEAD)[:, None] == jnp.arange(HALF)[None, :] % A_HEAD).astype(F32)
    hrows = -(-heads // (2 * SUBLANES)) * 2 * SUBLANES
    hmask = (jnp.arange(hrows)[:, None] == jnp.arange(width)[None, :] // A_HEAD).astype(F32)
    seq_blk = pl.BlockSpec((nb, tc, width), lambda b, c: (b, c, 0))
    st_blk = pl.BlockSpec((nb, A_HEAD, width), lambda b, c: (b, 0, 0))
    stage = (unroll, nb * nh, 2 * A_HEAD, HALF)
    o, s_new = pl.pallas_call(
        functools.partial(_wkv_kernel, unroll), grid=(n_seq // nb, seg_len // tc),
        in_specs=[seq_blk] * 6 + [st_blk, _const_spec(bb.shape), _const_spec(eye.shape), _const_spec(hmask.shape)],
        out_specs=[pl.BlockSpec((nb, tc, heads, A_HEAD), lambda b, c: (b, c, 0, 0)), st_blk],
        out_shape=[jax.ShapeDtypeStruct((n_seq, seg_len, heads, A_HEAD), F32),
                   jax.ShapeDtypeStruct((n_seq, A_HEAD, width), F32)],
        scratch_shapes=[pltpu.VMEM((nb, A_HEAD, width), F32), pltpu.VMEM((nb, A_HEAD, width), BF16),
                        pltpu.VMEM(stage, BF16), pltpu.VMEM(stage, F32)],
        compiler_params=_cparams(("arbitrary", "arbitrary")),
    )(*ops3, s0_l, bb, eye, hmask)
    s_new = jnp.transpose(s_new.reshape(n_seq, A_HEAD, heads, A_HEAD), (0, 2, 1, 3))
    return o.reshape(n_seq * seg_len, width), s_new


def _ab_out_kernel(o_ref, bonus_ref, g_ref, yb_ref, x_ref, lnw_ref, lnb_ref, wout_ref, bb_ref, y_ref):
    a_width = o_ref.shape[1]
    o = o_ref[...]
    bb = bb_ref[...]
    mean = _seg_sum(o, bb) * (1.0 / A_HEAD)
    dlt = o - mean
    var = _seg_sum(dlt * dlt, bb) * (1.0 / A_HEAD)
    yn = dlt * lax.rsqrt(var + GN_EPS) * lnw_ref[...] + lnb_ref[...]
    ya = (yn + bonus_ref[...]) * g_ref[...]
    y = _bdot(ya, wout_ref[:a_width, :]) + _bdot(yb_ref[...], wout_ref[a_width:, :])
    y_ref[...] = x_ref[...] + y


def _ab_out(o, bonus, g, yb, x, prm, tm):
    n, d = x.shape
    row = lambda c: pl.BlockSpec((tm, c), lambda i: (i, 0))
    params = [prm["lnw"], prm["lnb"], prm["wout"], prm["bb"]]
    return pl.pallas_call(
        _ab_out_kernel, grid=(n // tm,),
        in_specs=[row(o.shape[1]), row(bonus.shape[1]), row(g.shape[1]), row(yb.shape[1]), row(d)]
        + [_const_spec(p.shape) for p in params],
        out_specs=row(d), out_shape=jax.ShapeDtypeStruct((n, d), F32),
        compiler_params=_cparams(("arbitrary",)),
    )(o, bonus, g, yb, x, *params)


def _mem_kv_kernel(x_ref, gm_ref, wkv_ref, kg_ref, k_ref, v_ref):
    d = x_ref.shape[1]
    hd = d // X_HEADS
    h = _rms_rows(x_ref[...], gm_ref[...])
    kv = _bdot(h, wkv_ref[...])
    kg = kg_ref[...]
    for hh in range(X_HEADS):
        kh = kv[:, hh * hd:(hh + 1) * hd]
        k_ref[:, hh * hd:(hh + 1) * hd] = _rms_rows(kh, kg)
    v_ref[...] = kv[:, d:]


def _mem_kv(mem, gm, wkv, kg, tm):
    n, d = mem.shape
    row = pl.BlockSpec((tm, d), lambda i: (i, 0))
    return pl.pallas_call(
        _mem_kv_kernel, grid=(n // tm,),
        in_specs=[row, _const_spec(gm.shape), _const_spec(wkv.shape), _const_spec(kg.shape)],
        out_specs=[row, row], out_shape=[jax.ShapeDtypeStruct((n, d), F32)] * 2,
        compiler_params=_cparams(("arbitrary",)),
    )(mem, gm, wkv, kg)


def _cross_kernel(x_ref, mk_ref, mv_ref, gm_ref, wq_ref, qg_ref, wo_ref, y_ref, o_scr):
    d = x_ref.shape[1]
    hd = d // X_HEADS
    x = x_ref[...]
    q = _bdot(_rms_rows(x, gm_ref[...]), wq_ref[...])
    qg = qg_ref[...]
    scale = hd ** -0.5
    for hh in range(X_HEADS):
        sl = slice(hh * hd, (hh + 1) * hd)
        qh = _rms_rows(q[:, sl], qg)
        s = _bdot_nt(qh, mk_ref[0, :, sl]) * scale
        p = jnp.exp(s - jnp.max(s, axis=-1, keepdims=True))
        p = p / jnp.sum(p, axis=-1, keepdims=True)
        o_scr[:, sl] = _bdot(p, mv_ref[0, :, sl])
    y_ref[...] = x + _bdot(o_scr[...], wo_ref[...])


def _cross(x, seg_len, tm, mk, mv, gm, wq, qg, wo):
    n, d = x.shape
    tiles_per_seq = seg_len // tm
    row = pl.BlockSpec((tm, d), lambda i: (i, 0))
    mem = pl.BlockSpec((1, mk.shape[1], d), lambda i: (i // tiles_per_seq, 0, 0))
    params = [gm, wq, qg, wo]
    return pl.pallas_call(
        _cross_kernel, grid=(n // tm,),
        in_specs=[row, mem, mem] + [_const_spec(p.shape) for p in params],
        out_specs=row, out_shape=jax.ShapeDtypeStruct((n, d), F32),
        scratch_shapes=[pltpu.VMEM((tm, d), F32)],
        compiler_params=_cparams(("arbitrary",)),
    )(x, mk, mv, *params)


def _ffn_kernel(carry_mode, seg_len, tiles_per_seq, n_chunks, *refs):
    x_ref, gm_ref, wup_ref, cw_ref, wdn_ref = refs[:5]
    if carry_mode:
        init_ref, y_ref, gate_o, tail = refs[5:9]
    else:
        e0_ref, e1_ref, y_ref, gate_o = refs[5:9]
    tm = x_ref.shape[0]
    d_ff = cw_ref.shape[1]
    ck = d_ff // n_chunks
    rows = lax.broadcasted_iota(jnp.int32, (tm, 1), 0)

    if carry_mode:
        @pl.when(pl.program_id(0) % tiles_per_seq == 0)
        def _():
            tail[...] = init_ref[0]

    x = x_ref[...]
    h = _rms_rows(x, gm_ref[...]).astype(BF16)
    acc = x
    for c in range(n_chunks):
        sl = slice(c * ck, (c + 1) * ck)
        gate = jnp.dot(h, wup_ref[:, sl], preferred_element_type=F32)
        up = jnp.dot(h, wup_ref[:, d_ff + c * ck:d_ff + (c + 1) * ck], preferred_element_type=F32)
        if carry_mode:
            xm1, xm2 = _taps(gate, rows, seg_len, tail[:, sl], None, None)
        else:
            xm1, xm2 = _taps(gate, rows, seg_len, None, e0_ref[:, sl], e1_ref[:, sl])
        cv = cw_ref[0:1, sl] * xm2 + cw_ref[1:2, sl] * xm1 + cw_ref[2:3, sl] * gate
        act = cv * _sigmoid(cv) * up
        acc = acc + _bdot(act, wdn_ref[sl, :])
        if carry_mode:
            last = gate[tm - SUBLANES:, :]
            tail[:, sl] = last
            gate_o[0, :, sl] = last
        else:
            gate_o[:, sl] = gate
    y_ref[...] = acc


def _ffn(x, seg_len, tm, conv_prev, gm, wup, cw, wdn, n_chunks=2):
    n, d = x.shape
    d_ff = cw.shape[1]
    n_seq = n // seg_len
    carry_mode = seg_len >= tm
    tiles_per_seq = max(seg_len // tm, 1)
    row = lambda c: pl.BlockSpec((tm, c), lambda i: (i, 0))
    params = [gm, wup, cw, wdn]
    in_specs = [row(d)] + [_const_spec(p.shape) for p in params]
    if carry_mode:
        seq_blk = pl.BlockSpec((1, SUBLANES, d_ff), lambda i: (i // tiles_per_seq, 0, 0))
        extra = [_tail8(conv_prev)]
        in_specs += [seq_blk]
        out_specs = [row(d), seq_blk]
        out_shape = [jax.ShapeDtypeStruct((n, d), F32), jax.ShapeDtypeStruct((n_seq, SUBLANES, d_ff), F32)]
        scratch = [pltpu.VMEM((SUBLANES, d_ff), F32)]
    else:
        extra = [jnp.repeat(conv_prev[:, 0], seg_len, axis=0), jnp.repeat(conv_prev[:, 1], seg_len, axis=0)]
        in_specs += [row(d_ff), row(d_ff)]
        out_specs = [row(d), row(d_ff)]
        out_shape = [jax.ShapeDtypeStruct((n, d), F32), jax.ShapeDtypeStruct((n, d_ff), F32)]
        scratch = []
    y, gate_t = pl.pallas_call(
        functools.partial(_ffn_kernel, carry_mode, seg_len, tiles_per_seq, n_chunks),
        grid=(n // tm,), in_specs=in_specs, out_specs=out_specs, out_shape=out_shape, scratch_shapes=scratch,
        compiler_params=_cparams(("arbitrary",)),
    )(x, *params, *extra)
    if carry_mode:
        conv_new = gate_t[:, SUBLANES - (CONV_W - 1):]
    else:
        conv_new = gate_t.reshape(n_seq, seg_len, d_ff)[:, -(CONV_W - 1):]
    return y, conv_new


def _log_sigmoid(x):
    return jnp.minimum(x, 0.0) - jnp.log(1.0 + jnp.exp(-jnp.abs(x)))


def _fox_in_kernel(with_cum, tiles_per_seq, *refs):
    x_ref, gm_ref, wqg_ref, wkvt_ref, wfl_ref, fb_ref, qg_ref, kgc_ref, bb_ref = refs[:9]
    if with_cum:
        wflt_ref, fbt_ref, tri_ref = refs[9:12]
        q_o, kt_o, vt_o, og_o, lf_o, cum_o, run = refs[12:19]
    else:
        q_o, kt_o, vt_o, og_o, lf_o = refs[9:14]
    tm, d = x_ref.shape
    heads = lf_o.shape[1]
    h = _rms_rows(x_ref[...], gm_ref[...]).astype(BF16)
    proj = jnp.dot(h, wqg_ref[...], preferred_element_type=F32)
    q = proj[:, :d]
    q = q * lax.rsqrt(_seg_sum(q * q, bb_ref[...]) * (1.0 / C_HEAD) + RMS_EPS) * qg_ref[...]
    q_o[...] = (q * (C_HEAD ** -0.5 * LOG2E)).astype(q_o.dtype)
    og_o[...] = proj[:, d:]
    kvt = lax.dot_general(wkvt_ref[...], h, (((1,), (1,)), ((), ())), preferred_element_type=F32)
    k3 = kvt[:d].reshape(heads, C_HEAD, tm)
    ms = jnp.sum(k3 * k3, axis=1, keepdims=True) * (1.0 / C_HEAD)
    gain = jnp.broadcast_to(kgc_ref[...][:, :1], (C_HEAD, tm))
    kt_o[0] = (k3 * lax.rsqrt(ms + RMS_EPS) * gain[None]).reshape(d, tm)
    vt_o[0] = kvt[d:]
    fl = jnp.dot(h, wfl_ref[...], preferred_element_type=F32) + fb_ref[...]
    lf_o[...] = _log_sigmoid(fl)[:, :heads]

    if with_cum:
        @pl.when(pl.program_id(0) % tiles_per_seq == 0)
        def _():
            run[...] = jnp.zeros_like(run)

        flt = lax.dot_general(wflt_ref[...], h, (((1,), (1,)), ((), ())), preferred_element_type=F32)
        lft = _log_sigmoid(flt + fbt_ref[...])
        tri = tri_ref[...]
        cum = run[...][:, :1]
        for part in _split3(lft):
            cum = cum + jnp.dot(part, tri, preferred_element_type=F32)
        cum_o[0] = cum
        run[...] = jnp.broadcast_to(cum[:, cum.shape[1] - 1:], run.shape)


def _fox_in(x, seg_len, tm, prm, with_cum, q_dtype):
    n, d = x.shape
    n_seq = n // seg_len
    heads = d // C_HEAD
    tiles_per_seq = seg_len // tm
    row = lambda c: pl.BlockSpec((tm, c), lambda i: (i, 0))
    tmin = pl.BlockSpec((1, d, tm), lambda i: (i // tiles_per_seq, 0, i % tiles_per_seq))
    params = [prm["gm"], prm["wqg"], prm["wkvt"], prm["wfl"], prm["fb"], prm["qg"], prm["kgc"], prm["bb"]]
    out_shape = [jax.ShapeDtypeStruct((n, d), q_dtype)] + [jax.ShapeDtypeStruct((n_seq, d, seg_len), F32)] * 2 + \
                [jax.ShapeDtypeStruct((n, d), F32), jax.ShapeDtypeStruct((n, heads), F32)]
    out_specs = [row(d), tmin, tmin, row(d), row(heads)]
    scratch = []
    if with_cum:
        params += [prm["wflt"], prm["fbt"], prm["tri"]]
        out_shape += [jax.ShapeDtypeStruct((n_seq, heads, seg_len), F32)]
        out_specs += [pl.BlockSpec((1, heads, tm), lambda i: (i // tiles_per_seq, 0, i % tiles_per_seq))]
        scratch = [pltpu.VMEM((heads, LANES), F32)]
    return pl.pallas_call(
        functools.partial(_fox_in_kernel, with_cum, tiles_per_seq),
        grid=(n // tm,), in_specs=[row(d)] + [_const_spec(p.shape) for p in params],
        out_specs=out_specs, out_shape=out_shape, scratch_shapes=scratch,
        compiler_params=_cparams(("arbitrary",)),
    )(x, *params)


def _fox_attn_kernel(tk, q_ref, k_ref, v_ref, cum_ref, o_ref, acc, mrun, s_a, s_b):
    tq = q_ref.shape[0]
    pair = pl.program_id(1)
    qi = pl.program_id(2)
    lane = lax.broadcasted_iota(jnp.int32, (1, LANES), 1)
    first = lane < C_HEAD
    q = q_ref[...]
    zero = jnp.zeros_like(q)
    qs = (jnp.where(first, q, zero), jnp.where(first, zero, q))
    acc[...] = jnp.zeros_like(acc)
    mrun[...] = jnp.full_like(mrun, NEG)

    top = lax.broadcasted_iota(jnp.int32, (LANES, 1), 0) < C_HEAD

    def cols(j):
        return pl.ds(pl.multiple_of(j * tk, tk), tk)

    def qk(j, s_ref):
        kb = k_ref[0, :, cols(j)].astype(BF16)
        for hh in range(2):
            s_ref[hh] = jnp.dot(qs[hh], kb, preferred_element_type=F32)

    def softmax_pv(j, s_ref, masked):
        vb = v_ref[0, :, cols(j)].astype(BF16)
        one = jnp.ones_like(vb)
        vs = (jnp.where(top, vb, one), jnp.where(top, one, vb))
        for hh in range(2):
            bias = cum_ref[0, 2 * pair + hh, pl.ds(j, 1), :] * (-LOG2E)
            s = s_ref[hh] + bias
            if masked:
                rr = lax.broadcasted_iota(jnp.int32, (tq, tk), 0)
                cc = lax.broadcasted_iota(jnp.int32, (tq, tk), 1)
                s = jnp.where(cc <= rr, s, NEG)
            m_old = mrun[hh]
            m_new = jnp.maximum(m_old, jnp.max(s, axis=-1, keepdims=True))
            p = jnp.exp2(s - m_new[:, :1])
            acc[hh] = jnp.exp2(m_old - m_new) * acc[hh] + _bdot_nt(p, vs[hh])
            mrun[hh] = m_new

    pairs = qi // 2
    qk(0, s_a)

    def body(jj, carry):
        qk(2 * jj + 1, s_b)
        softmax_pv(2 * jj, s_a, False)
        qk(2 * jj + 2, s_a)
        softmax_pv(2 * jj + 1, s_b, False)
        return carry

    lax.fori_loop(0, pairs, body, 0)

    @pl.when(qi == 2 * pairs)
    def _():
        softmax_pv(qi, s_a, True)

    @pl.when(qi != 2 * pairs)
    def _():
        qk(qi, s_b)
        softmax_pv(qi - 1, s_a, False)
        softmax_pv(qi, s_b, True)

    a0 = acc[0]
    a1 = acc[1]
    o_ref[...] = jnp.where(first, a0 / pltpu.roll(a0, C_HEAD, 1), a1 / pltpu.roll(a1, C_HEAD, 1))


def _fox_attn(q, kt, vt, cum_t, n_seq, seg_len, tq):
    n, d = q.shape
    heads = d // C_HEAD
    nq = seg_len // tq
    cum4 = cum_t.reshape(n_seq, heads, nq, tq)
    qblk = pl.BlockSpec((tq, LANES), lambda b, p, i: (b * nq + i, p))
    kblk = pl.BlockSpec((1, LANES, seg_len), lambda b, p, i: (b, p, 0))
    cblk = pl.BlockSpec((1, heads, nq, tq), lambda b, p, i: (b, 0, 0, 0))
    return pl.pallas_call(
        functools.partial(_fox_attn_kernel, tq),
        grid=(n_seq, d // LANES, nq),
        in_specs=[qblk, kblk, kblk, cblk],
        out_specs=qblk, out_shape=jax.ShapeDtypeStruct((n, d), F32),
        scratch_shapes=[pltpu.VMEM((2, tq, LANES), F32), pltpu.VMEM((2, tq, LANES), F32),
                        pltpu.VMEM((2, tq, tq), F32), pltpu.VMEM((2, tq, tq), F32)],
        compiler_params=_cparams(("arbitrary", "arbitrary", "arbitrary")),
    )(q, kt, vt, cum4)


def _fox_out_kernel(o_ref, og_ref, x_ref, w_ref, y_ref):
    y_ref[...] = x_ref[...] + _bdot(o_ref[...] * _sigmoid(og_ref[...]), w_ref[...])


def _fox_out(o, og, x, w, tm):
    n, d = x.shape
    row = pl.BlockSpec((tm, d), lambda i: (i, 0))
    return pl.pallas_call(
        _fox_out_kernel, grid=(n // tm,), in_specs=[row, row, row, _const_spec(w.shape)],
        out_specs=row, out_shape=jax.ShapeDtypeStruct((n, d), F32),
        compiler_params=_cparams(("arbitrary",)),
    )(o, og, x, w)


def _paged_kernel(pps, n_new, *refs):
    pt_ref = refs[0]
    q_ref = refs[1]
    k_refs = refs[2:2 + pps]
    v_refs = refs[2 + pps:2 + 2 * pps]
    lf_refs = refs[2 + 2 * pps:2 + 3 * pps]
    kn_ref, vn_ref, lfn_ref, hsel_ref, tri_ref, o_ref = refs[2 + 3 * pps:8 + 3 * pps]
    qexp, acc, mrun, lrun, crun, kt_all, vt_all, lf_all = refs[8 + 3 * pps:]
    del pt_ref
    step = pl.program_id(1)
    last = pl.num_programs(1) - 1
    rows, d = qexp.shape
    heads = rows // n_new
    rhead = _idiv(lax.broadcasted_iota(jnp.int32, (rows, 1), 0), n_new)
    lhead = _idiv(lax.broadcasted_iota(jnp.int32, (1, d), 1), C_HEAD)

    @pl.when(step == 0)
    def _():
        qf = q_ref[...].astype(F32)
        qt = jnp.concatenate([qf] * heads, axis=0)
        qexp[...] = jnp.where(rhead == lhead, qt, 0.0).astype(BF16)
        acc[...] = jnp.zeros_like(acc)
        mrun[...] = jnp.full_like(mrun, NEG)
        lrun[...] = jnp.zeros_like(lrun)
        crun[...] = jnp.zeros_like(crun)

    def block(kt, vt, lft, npg, new_tokens):
        s = jnp.dot(qexp[...], kt, preferred_element_type=F32)
        parts = [jnp.dot(hsel_ref[...], part, preferred_element_type=F32).astype(BF16) for part in _split3(lft)]
        stacked = jnp.concatenate([part[:, i * PAGE:(i + 1) * PAGE] for part in parts for i in range(npg)], axis=0)
        loc = jnp.dot(stacked, tri_ref[...], preferred_element_type=F32)
        off = crun[...]
        cols = []
        for i in range(npg):
            cum = off
            for pi in range(len(parts)):
                cum = cum + loc[(pi * npg + i) * rows:(pi * npg + i + 1) * rows, :]
            cols.append(s[:, i * PAGE:(i + 1) * PAGE] - cum * LOG2E)
            off = jnp.broadcast_to(cum[:, PAGE - 1:], (rows, PAGE))
        crun[...] = off
        s = cols[0] if npg == 1 else jnp.concatenate(cols, axis=1)
        if new_tokens:
            qpos = _pos_in_seq(lax.broadcasted_iota(jnp.int32, (rows, PAGE), 0), n_new)
            kpos = lax.broadcasted_iota(jnp.int32, (rows, PAGE), 1)
            s = jnp.where(kpos <= qpos, s, NEG)
        m_old = mrun[...][:, :1]
        m_new = jnp.maximum(m_old, jnp.max(s, axis=-1, keepdims=True))
        alpha = jnp.exp2(m_old - m_new)
        p = jnp.exp2(s - m_new)
        lrun[...] = alpha * lrun[...] + jnp.sum(p, axis=-1, keepdims=True)
        acc[...] = alpha * acc[...] + _bdot_nt(p, vt)
        mrun[...] = jnp.broadcast_to(m_new, mrun.shape)

    @pl.when(step < last)
    def _():
        for i in range(pps):
            kt_all[:, i * PAGE:(i + 1) * PAGE] = k_refs[i][0, 0].astype(BF16)
            vt_all[:, i * PAGE:(i + 1) * PAGE] = v_refs[i][0, 0].astype(BF16)
            lf_all[:, i * PAGE:(i + 1) * PAGE] = lf_refs[i][0, 0]
        block(kt_all[...], vt_all[...], lf_all[...], pps, False)

    @pl.when(step == last)
    def _():
        block(kn_ref[0].astype(BF16), vn_ref[0].astype(BF16), lfn_ref[0], 1, True)
        res = acc[...] / lrun[...][:, :1]
        out = jnp.zeros((n_new, d), F32)
        for hh in range(heads):
            out = out + jnp.where(lhead == hh, res[hh * n_new:(hh + 1) * n_new, :], 0.0)
        o_ref[...] = out


def _paged_attn(q, kt_new, vt_new, lf_new, cache_k, cache_v, cache_lf, layer, page_table, n_seq, n_new, pps):
    n, d = q.shape
    heads = d // C_HEAD
    rows = heads * n_new
    n_pages = page_table.shape[1]
    steps = n_pages // pps
    nc, n_pool = cache_k.shape[:2]
    ck = jnp.transpose(cache_k, (0, 1, 3, 4, 2)).reshape(nc, n_pool, d, PAGE)
    cv = jnp.transpose(cache_v, (0, 1, 3, 4, 2)).reshape(nc, n_pool, d, PAGE)
    cache_lf = jnp.transpose(cache_lf, (0, 1, 3, 2))
    pad = lambda at: jnp.pad(jnp.transpose(at.reshape(at.shape[0], n_seq, n_new), (1, 0, 2)),
                             ((0, 0), (0, 0), (0, PAGE - n_new)))
    hsel = (jnp.arange(rows)[:, None] // n_new == jnp.arange(heads)[None, :]).astype(BF16)
    tri = (jnp.arange(PAGE)[:, None] <= jnp.arange(PAGE)[None, :]).astype(BF16)

    def page_spec(i, width):
        return pl.BlockSpec((1, 1, width, PAGE),
                            lambda b, s, pt: (layer, pt[b, jnp.minimum(s, steps - 1) * pps + i], 0, 0))

    new_spec = lambda width: pl.BlockSpec((1, width, PAGE), lambda b, s, pt: (b, 0, 0))
    const = lambda shape: pl.BlockSpec(shape, lambda b, s, pt: (0, 0))
    qspec = pl.BlockSpec((n_new, d), lambda b, s, pt: (b, 0))
    grid_spec = pltpu.PrefetchScalarGridSpec(
        num_scalar_prefetch=1, grid=(n_seq, steps + 1),
        in_specs=[qspec] + [page_spec(i, d) for i in range(pps)] * 2 + [page_spec(i, heads) for i in range(pps)]
        + [new_spec(d), new_spec(d), new_spec(heads), const(hsel.shape), const(tri.shape)],
        out_specs=qspec,
        scratch_shapes=[pltpu.VMEM((rows, d), BF16), pltpu.VMEM((rows, d), F32), pltpu.VMEM((rows, PAGE), F32),
                        pltpu.VMEM((rows, PAGE), F32), pltpu.VMEM((rows, PAGE), F32),
                        pltpu.VMEM((d, pps * PAGE), BF16), pltpu.VMEM((d, pps * PAGE), BF16),
                        pltpu.VMEM((heads, pps * PAGE), F32)])
    return pl.pallas_call(
        functools.partial(_paged_kernel, pps, n_new), grid_spec=grid_spec,
        out_shape=jax.ShapeDtypeStruct((n, d), F32),
        compiler_params=_cparams(("arbitrary", "arbitrary")),
    )(page_table, q, *([ck] * pps), *([cv] * pps), *([cache_lf] * pps), pad(kt_new), pad(vt_new), pad(lf_new.T),
      hsel, tri)


def _block_ones(width, block):
    i = jnp.arange(width)
    return (i[:, None] // block == i[None, :] // block).astype(BF16)


def kernel(x_prompt, x_sample, mem_prompt, state_wkv, state_shift, state_conv, state_ffn_conv, cache_k, cache_v, cache_logf, cache_mem_k, cache_mem_v, page_table, norm_mix, norm_cross, norm_mem, norm_ffn, ab_in, ab_out, rw_mu, rw_w0, rw_w2, rw_a0, rw_a2, rw_g2, rw_kk, rw_ka, rw_rk, rw_lnw, rw_lnb, sc_conv_w, fox_in, fox_out, fox_qg, fox_kg, fox_fb, xq_w, xkv_w, xo_w, xq_g, xk_g, ffn_up, ffn_conv_w, ffn_down):
    bp, tp, d = x_prompt.shape
    bs, ts, _ = x_sample.shape
    n_mem = mem_prompt.shape[1]
    depth = norm_mix.shape[0]
    heads_c = d // C_HEAD
    a_width = rw_w0.shape[1]
    b_width = sc_conv_w.shape[2]
    d_ff = ffn_conv_w.shape[2]
    np_, ns_ = bp * tp, bs * ts
    tm_p = min(512, tp)
    row2 = lambda a: a.reshape(1, -1).astype(F32)
    bb = _block_ones(HALF, A_HEAD)

    yp = x_prompt.reshape(np_, d)
    ys = x_sample.reshape(ns_, d)
    mem = mem_prompt.reshape(bp * n_mem, d)
    pw, psh, pcv, pfc, pk, pv, plf, pmk, pmv = [], [], [], [], [], [], [], [], []
    sw, ssh, scv, sfc, sk, sv, slf = [], [], [], [], [], [], []

    for layer in range(depth):
        i = layer // 2
        if layer % 2 == 0:
            zrow = jnp.zeros((LANES // 2, a_width), F32)
            prm = dict(
                gm=row2(norm_mix[layer]), win=ab_in[i].astype(BF16), mu=row2(rw_mu[i]), w0=row2(rw_w0[i]),
                w2p=jnp.concatenate([rw_w2[i], zrow], axis=0).astype(BF16), a0=row2(rw_a0[i]),
                a2p=jnp.concatenate([zrow, rw_a2[i]], axis=0).astype(BF16), g2=rw_g2[i].astype(BF16),
                kkw=row2(rw_kk[i]), ka=row2(rw_ka[i]), rk=row2(rw_rk[i]), cw=sc_conv_w[i].astype(F32), bb=bb,
                lnw=row2(rw_lnw[i]), lnb=row2(rw_lnb[i]), wout=ab_out[i].astype(BF16))
            a_proj = ab_in.shape[2] - 3 * b_width
            ops, bonus, g, yb, sh, cv = _ab_in(yp, tp, tm_p, jnp.zeros((bp, a_proj), F32),
                                               jnp.zeros((bp, CONV_W - 1, b_width), F32), prm)
            o, st = _wkv(ops, bp, tp, jnp.zeros((bp, a_width // A_HEAD, A_HEAD, A_HEAD), F32), bb,
                         nb=min(4, bp), tc=min(128, tp), unroll=8)
            yp = _ab_out(o, bonus, g, yb, yp, prm, tm_p)
            pw.append(st); psh.append(sh); pcv.append(cv)
            ops, bonus, g, yb, sh, cv = _ab_in(ys, ts, ns_, state_shift[i], state_conv[i], prm)
            o, st = _wkv(ops, bs, ts, state_wkv[i], bb, nb=min(4, bs), tc=ts, unroll=8)
            ys = _ab_out(o, bonus, g, yb, ys, prm, ns_)
            sw.append(st); ssh.append(sh); scv.append(cv)
        else:
            w = fox_in[i]
            wfl = w[:, 3 * d:3 * d + heads_c]
            prm = dict(
                gm=row2(norm_mix[layer]),
                wqg=jnp.concatenate([w[:, :d], w[:, 3 * d + heads_c:]], axis=1).astype(BF16),
                wkvt=w[:, d:3 * d].T.astype(BF16),
                wfl=jnp.pad(wfl, ((0, 0), (0, LANES - heads_c))).astype(BF16),
                fb=jnp.pad(row2(fox_fb[i]), ((0, 0), (0, LANES - heads_c))),
                qg=row2(jnp.tile(fox_qg[i], heads_c)),
                kgc=jnp.broadcast_to(fox_kg[i].astype(F32)[:, None], (C_HEAD, LANES)), bb=bb,
                wflt=wfl.T.astype(BF16), fbt=fox_fb[i].reshape(heads_c, 1).astype(F32),
                tri=(jnp.arange(tm_p)[:, None] <= jnp.arange(tm_p)[None, :]).astype(BF16))
            wo = fox_out[i].astype(BF16)
            tok_major = lambda zt, b, t: jnp.transpose(zt.reshape(b, heads_c, C_HEAD, t), (0, 3, 1, 2))
            q, kt, vt, og, lf, cum_t = _fox_in(yp, tp, tm_p, prm, True, BF16)
            o = _fox_attn(q, kt, vt, cum_t, bp, tp, tm_p)
            yp = _fox_out(o, og, yp, wo, tm_p)
            pk.append(tok_major(kt, bp, tp)); pv.append(tok_major(vt, bp, tp))
            plf.append(lf.reshape(bp, tp, heads_c))
            q, kt, vt, og, lf = _fox_in(ys, ns_, ns_, prm, False, F32)
            o = _paged_attn(q, kt[0], vt[0], lf, cache_k, cache_v, cache_logf, i, page_table, bs, ts, pps=8)
            ys = _fox_out(o, og, ys, wo, ns_)
            sk.append(tok_major(kt, 1, ns_).reshape(bs, ts, heads_c, C_HEAD))
            sv.append(tok_major(vt, 1, ns_).reshape(bs, ts, heads_c, C_HEAD))
            slf.append(lf.reshape(bs, ts, heads_c))

        wq, wo_x = xq_w[layer].astype(BF16), xo_w[layer].astype(BF16)
        mk, mv = _mem_kv(mem, row2(norm_mem[layer]), xkv_w[layer].astype(BF16), row2(xk_g[layer]), n_mem)
        pmk.append(mk.reshape(bp, n_mem, X_HEADS, d // X_HEADS)); pmv.append(mv.reshape(bp, n_mem, X_HEADS, d // X_HEADS))
        gx, qg = row2(norm_cross[layer]), row2(xq_g[layer])
        yp = _cross(yp, tp, tm_p, mk.reshape(bp, n_mem, d), mv.reshape(bp, n_mem, d), gx, wq, qg, wo_x)
        ys = _cross(ys, ts, ts, cache_mem_k[layer].reshape(bs, n_mem, d), cache_mem_v[layer].reshape(bs, n_mem, d),
                    gx, wq, qg, wo_x)

        gf, wup, cwf, wdn = row2(norm_ffn[layer]), ffn_up[layer].astype(BF16), ffn_conv_w[layer].astype(F32), \
            ffn_down[layer].astype(BF16)
        yp, cvp = _ffn(yp, tp, tm_p, jnp.zeros((bp, CONV_W - 1, d_ff), F32), gf, wup, cwf, wdn)
        ys, cvs = _ffn(ys, ts, ns_, state_ffn_conv[layer], gf, wup, cwf, wdn)
        pfc.append(cvp); sfc.append(cvs)

    st = jnp.stack
    return (yp.reshape(bp, tp, d), ys.reshape(bs, ts, d), st(pw), st(psh), st(pcv), st(pfc), st(pk), st(pv), st(plf),
            st(pmk), st(pmv), st(sw), st(ssh), st(scv), st(sfc), st(sk), st(sv), st(slf))
```

```python
import functools

import jax
import jax.numpy as jnp
from jax import lax
from jax.experimental import pallas as pl
from jax.experimental.pallas import tpu as pltpu

F32 = jnp.float32
BF16 = jnp.bfloat16

RMS_EPS = 1e-6
GN_EPS = 64e-5
L2_EPS = 1e-12
CONV_W = 3
A_HEAD = 64
C_HEAD = 64
X_HEADS = 4
PAGE = 128
LANES = 128
SUBLANES = 8
HALF = 2 * LANES
VMEM_LIMIT = 56 * 1024 * 1024
NEG = -1e30
LOG2E = 1.4426950408889634


def _bdot(a, b):
    return jnp.dot(a.astype(BF16), b.astype(BF16), preferred_element_type=F32)


def _bdot_nt(a, b):
    return lax.dot_general(a.astype(BF16), b.astype(BF16), (((1,), (1,)), ((), ())), preferred_element_type=F32)


def _split3(x):
    x1 = x.astype(BF16)
    r1 = x - x1.astype(F32)
    x2 = r1.astype(BF16)
    x3 = (r1 - x2.astype(F32)).astype(BF16)
    return x1, x2, x3


def _seg_sum(x, bb):
    w = bb.shape[0]
    outs = []
    for c in range(x.shape[1] // w):
        xc = x[:, c * w:(c + 1) * w]
        hi = xc.astype(BF16)
        lo = (xc - hi.astype(F32)).astype(BF16)
        outs.append(jnp.dot(hi, bb, preferred_element_type=F32) + jnp.dot(lo, bb, preferred_element_type=F32))
    return outs[0] if len(outs) == 1 else jnp.concatenate(outs, axis=1)


def _seg_sum1(xb, bb):
    w = bb.shape[0]
    outs = [jnp.dot(xb[:, c * w:(c + 1) * w], bb, preferred_element_type=F32) for c in range(xb.shape[1] // w)]
    return outs[0] if len(outs) == 1 else jnp.concatenate(outs, axis=1)


def _rms_rows(x, g):
    return x * lax.rsqrt(jnp.mean(x * x, axis=-1, keepdims=True) + RMS_EPS) * g


def _sigmoid(x):
    return 1.0 / (1.0 + jnp.exp(-x))


def _softplus(x):
    return jnp.maximum(x, 0.0) + jnp.log(1.0 + jnp.exp(-jnp.abs(x)))


def _pos_in_seq(rows, seg_len):
    if seg_len & (seg_len - 1) == 0:
        return jnp.bitwise_and(rows, seg_len - 1)
    return lax.rem(rows, seg_len)


def _idiv(x, n):
    if n & (n - 1) == 0:
        return jnp.right_shift(x, n.bit_length() - 1)
    return lax.div(x, n)


def _taps(x, rows, seg_len, tail, e0, e1, want2=True):
    r1 = pltpu.roll(x, 1, 0)
    r2 = pltpu.roll(x, 2, 0) if want2 else None
    if tail is not None:
        c1 = tail[SUBLANES - 1:SUBLANES, :]
        xm1 = jnp.where(rows == 0, c1, r1)
        xm2 = None
        if want2:
            c0 = tail[SUBLANES - 2:SUBLANES - 1, :]
            xm2 = jnp.where(rows == 0, c0, jnp.where(rows == 1, c1, r2))
        return xm1, xm2
    pos = _pos_in_seq(rows, seg_len)
    xm1 = jnp.where(pos == 0, e1, r1)
    xm2 = jnp.where(pos == 0, e0, jnp.where(pos == 1, e1, r2)) if want2 else None
    return xm1, xm2


def _cparams(sem, vmem=VMEM_LIMIT):
    return pltpu.CompilerParams(dimension_semantics=sem, vmem_limit_bytes=vmem)


def _const_spec(shape):
    nd = len(shape)
    return pl.BlockSpec(shape, lambda *_: (0,) * nd, pipeline_mode=pl.Buffered(1))


def _ab_in_kernel(carry_mode, seg_len, tiles_per_seq, a_width, b_width, *refs):
    (x_ref, gm_ref, win_ref, mu_ref, w0_ref, w2_ref, a0_ref, a2_ref, g2_ref, kkw_ref, ka_ref, rk_ref,
     cw_ref, bb_ref) = refs[:14]
    if carry_mode:
        init_pa_ref, init_sq_ref = refs[14:16]
        outs = refs[16:27]
        pa_tail, sq_tail = refs[27:29]
    else:
        e_pa_ref, e_sq0_ref, e_sq1_ref = refs[14:17]
        outs = refs[17:28]
    (r_o, w_o, k_o, v_o, kk_o, b_o, bonus_o, g_o, yb_o, pa_o, sq_o) = outs
    tm = x_ref.shape[0]
    a_proj = 3 * a_width + 2 * LANES
    rows = lax.broadcasted_iota(jnp.int32, (tm, 1), 0)

    if carry_mode:
        @pl.when(pl.program_id(0) % tiles_per_seq == 0)
        def _():
            pa_tail[...] = init_pa_ref[0]
            sq_tail[...] = init_sq_ref[0]

    x = x_ref[...]
    h = _rms_rows(x, gm_ref[...]).astype(BF16)
    proj = jnp.dot(h, win_ref[...], preferred_element_type=F32)
    pa = proj[:, :a_proj]
    pb = proj[:, a_proj:]

    if carry_mode:
        prev, _ = _taps(pa, rows, seg_len, pa_tail[...], None, None, want2=False)
    else:
        prev, _ = _taps(pa, rows, seg_len, None, None, e_pa_ref[...], want2=False)
    xs = pa + (prev - pa) * mu_ref[...]
    r = xs[:, :a_width]
    k = xs[:, a_width:2 * a_width]
    v = xs[:, 2 * a_width:3 * a_width]
    wa = xs[:, 3 * a_width:3 * a_width + LANES]
    gd = xs[:, 3 * a_width + LANES:a_proj]
    wl = w0_ref[...] + _bdot(jnp.tanh(wa), w2_ref[...])
    w = -_softplus(-wl) - 0.5
    decay = jnp.exp(-jnp.exp(w))
    a = _sigmoid(a0_ref[...] + _bdot(wa, a2_ref[...]))
    g = _bdot(_sigmoid(gd), g2_ref[...])
    bb = bb_ref[...]
    kk = k * kkw_ref[...]
    kk = kk / jnp.maximum(jnp.sqrt(_seg_sum(kk * kk, bb)), L2_EPS)
    k2 = k * (1.0 + (a - 1.0) * ka_ref[...])
    bonus = _seg_sum(r * k2 * rk_ref[...], bb) * v
    r_o[...] = r
    w_o[...] = decay
    k_o[...] = k2
    v_o[...] = v
    kk_o[...] = kk
    b_o[...] = kk * a
    bonus_o[...] = bonus
    g_o[...] = g

    gb = pb[:, :b_width]
    sq = pb[:, b_width:2 * b_width] * pb[:, 2 * b_width:]
    if carry_mode:
        xm1, xm2 = _taps(sq, rows, seg_len, sq_tail[...], None, None)
    else:
        xm1, xm2 = _taps(sq, rows, seg_len, None, e_sq0_ref[...], e_sq1_ref[...])
    cw = cw_ref[...]
    yb_o[...] = gb * (cw[0:1, :] * xm2 + cw[1:2, :] * xm1 + cw[2:3, :] * sq)

    if carry_mode:
        pa_last = pa[tm - SUBLANES:, :]
        sq_last = sq[tm - SUBLANES:, :]
        pa_tail[...] = pa_last
        sq_tail[...] = sq_last
        pa_o[0] = pa_last
        sq_o[0] = sq_last
    else:
        pa_o[...] = pa
        sq_o[...] = sq


def _tail8(state):
    b, n, c = state.shape
    return jnp.concatenate([jnp.zeros((b, SUBLANES - n, c), state.dtype), state], axis=1)


def _ab_in(x, seg_len, tm, shift_prev, conv_prev, prm):
    n, d = x.shape
    n_seq = n // seg_len
    a_width = prm["w0"].shape[1]
    b_width = prm["cw"].shape[1]
    a_proj = 3 * a_width + 2 * LANES
    carry_mode = seg_len >= tm
    tiles_per_seq = max(seg_len // tm, 1)
    grid = (n // tm,)
    row = lambda c: pl.BlockSpec((tm, c), lambda i: (i, 0))
    params = [prm["gm"], prm["win"], prm["mu"], prm["w0"], prm["w2p"], prm["a0"], prm["a2p"], prm["g2"],
              prm["kkw"], prm["ka"], prm["rk"], prm["cw"], prm["bb"]]
    in_specs = [row(d)] + [_const_spec(p.shape) for p in params]
    out_shape = [jax.ShapeDtypeStruct((n, a_width), F32)] * 8 + [jax.ShapeDtypeStruct((n, b_width), F32)]
    out_specs = [row(a_width)] * 8 + [row(b_width)]
    if carry_mode:
        seq_blk = lambda c: pl.BlockSpec((1, SUBLANES, c), lambda i: (i // tiles_per_seq, 0, 0))
        extra = [_tail8(shift_prev[:, None, :]), _tail8(conv_prev)]
        in_specs += [seq_blk(a_proj), seq_blk(b_width)]
        out_shape += [jax.ShapeDtypeStruct((n_seq, SUBLANES, a_proj), F32),
                      jax.ShapeDtypeStruct((n_seq, SUBLANES, b_width), F32)]
        out_specs += [seq_blk(a_proj), seq_blk(b_width)]
        scratch = [pltpu.VMEM((SUBLANES, a_proj), F32), pltpu.VMEM((SUBLANES, b_width), F32)]
    else:
        extra = [jnp.repeat(shift_prev, seg_len, axis=0), jnp.repeat(conv_prev[:, 0], seg_len, axis=0),
                 jnp.repeat(conv_prev[:, 1], seg_len, axis=0)]
        in_specs += [row(a_proj), row(b_width), row(b_width)]
        out_shape += [jax.ShapeDtypeStruct((n, a_proj), F32), jax.ShapeDtypeStruct((n, b_width), F32)]
        out_specs += [row(a_proj), row(b_width)]
        scratch = []
    outs = pl.pallas_call(
        functools.partial(_ab_in_kernel, carry_mode, seg_len, tiles_per_seq, a_width, b_width),
        grid=grid, in_specs=in_specs, out_specs=out_specs, out_shape=out_shape, scratch_shapes=scratch,
        compiler_params=_cparams(("arbitrary",)),
    )(x, *params, *extra)
    scan_ops = outs[:6]
    bonus, g, yb, pa_t, sq_t = outs[6:]
    if carry_mode:
        shift_new = pa_t[:, SUBLANES - 1]
        conv_new = sq_t[:, SUBLANES - (CONV_W - 1):]
    else:
        shift_new = pa_t.reshape(n_seq, seg_len, a_proj)[:, -1]
        conv_new = sq_t.reshape(n_seq, seg_len, b_width)[:, -(CONV_W - 1):]
    return scan_ops, bonus, g, yb, shift_new, conv_new


def _wkv_kernel(unroll, r_ref, w_ref, k_ref, v_ref, kk_ref, b_ref, s0_ref, bb_ref, eye_ref, hmask_ref,
                o_ref, s_out_ref, s_scr, s_bf, lhs, res):
    nb, tc, width = r_ref.shape
    nh = width // HALF
    heads = width // A_HEAD
    rc = 2 * SUBLANES
    chunks = range(A_HEAD // rc)
    bb = bb_ref[...]

    def rows_of(c):
        return slice(c * rc, (c + 1) * rc)

    def stage_rows(kind, c):
        return slice(kind * A_HEAD + c * rc, kind * A_HEAD + (c + 1) * rc)

    def build(b, h, t, slot):
        ls = slice(h * HALF, (h + 1) * HALF)
        kk, v = kk_ref[b, pl.ds(t, 1), ls], v_ref[b, pl.ds(t, 1), ls].astype(BF16)
        for c in chunks:
            lhs[slot, b * nh + h, stage_rows(0, c), :] = (s_scr[b, rows_of(c), ls] * kk).astype(BF16)
            lhs[slot, b * nh + h, stage_rows(1, c), :] = eye_ref[rows_of(c), :] * v
        res[slot, b * nh + h] = jnp.dot(lhs[slot, b * nh + h], bb, preferred_element_type=F32)

    def update(b, h, t, slot):
        ls = slice(h * HALF, (h + 1) * HALF)
        row = lambda ref: ref[b, pl.ds(t, 1), ls]
        w, bv, k = row(w_ref), row(b_ref), row(k_ref)
        for c in chunks:
            s = (s_scr[b, rows_of(c), ls] * w - res[slot, b * nh + h, stage_rows(0, c), :] * bv
                 + res[slot, b * nh + h, stage_rows(1, c), :] * k)
            s_scr[b, rows_of(c), ls] = s
            s_bf[b, rows_of(c), ls] = s.astype(BF16)

    def readout(b, t):
        rmat = (hmask_ref[...] * r_ref[b, pl.ds(t, 1), :]).astype(BF16)
        y = _bdot_nt(rmat, s_bf[b])
        o_ref[b, pl.ds(t, 1)] = y[:heads][None]

    @pl.when(pl.program_id(1) == 0)
    def _():
        s_scr[...] = s0_ref[...]

    for b in range(nb):
        for h in range(nh):
            build(b, h, 0, unroll - 1)

    def step(i, carry):
        for u in range(unroll):
            t = i * unroll + u
            nxt = jnp.minimum(t + 1, tc - 1)
            for b in range(nb):
                for h in range(nh):
                    update(b, h, t, (u - 1) % unroll)
                    build(b, h, nxt, u)
                readout(b, t)
        return carry

    lax.fori_loop(0, tc // unroll, step, 0)

    @pl.when(pl.program_id(1) == pl.num_programs(1) - 1)
    def _():
        s_out_ref[...] = s_scr[...]


def _wkv(scan_ops, n_seq, seg_len, s0, bb, nb, tc, unroll):
    width = scan_ops[0].shape[1]
    heads = width // A_HEAD
    nh = width // HALF
    ops3 = [a.reshape(n_seq, seg_len, width) for a in scan_ops]
    s0_l = jnp.transpose(s0.astype(F32), (0, 2, 1, 3)).reshape(n_seq, A_HEAD, width)
    eye = (jnp.arange(A_HEAD)[:, None] == jnp.arange(HALF)[None, :] % A_HEAD).astype(BF16)
    hrows = -(-heads // (2 * SUBLANES)) * 2 * SUBLANES
    hmask = (jnp.arange(hrows)[:, None] == jnp.arange(width)[None, :] // A_HEAD).astype(F32)
    seq_blk = pl.BlockSpec((nb, tc, width), lambda b, c: (b, c, 0))
    st_blk = pl.BlockSpec((nb, A_HEAD, width), lambda b, c: (b, 0, 0))
    stage = (unroll, nb * nh, 2 * A_HEAD, HALF)
    o, s_new = pl.pallas_call(
        functools.partial(_wkv_kernel, unroll), grid=(n_seq // nb, seg_len // tc),
        in_specs=[seq_blk] * 6 + [st_blk, _const_spec(bb.shape), _const_spec(eye.shape), _const_spec(hmask.shape)],
        out_specs=[pl.BlockSpec((nb, tc, heads, A_HEAD), lambda b, c: (b, c, 0, 0)), st_blk],
        out_shape=[jax.ShapeDtypeStruct((n_seq, seg_len, heads, A_HEAD), F32),
                   jax.ShapeDtypeStruct((n_seq, A_HEAD, width), F32)],
        scratch_shapes=[pltpu.VMEM((nb, A_HEAD, width), F32), pltpu.VMEM((nb, A_HEAD, width), BF16),
                        pltpu.VMEM(stage, BF16), pltpu.VMEM(stage, F32)],
        compiler_params=_cparams(("arbitrary", "arbitrary")),
    )(*ops3, s0_l, bb, eye, hmask)
    s_new = jnp.transpose(s_new.reshape(n_seq, A_HEAD, heads, A_HEAD), (0, 2, 1, 3))
    return o.reshape(n_seq * seg_len, width), s_new


def _ab_out_kernel(o_ref, bonus_ref, g_ref, yb_ref, x_ref, lnw_ref, lnb_ref, wout_ref, bb_ref, y_ref):
    a_width = o_ref.shape[1]
    o = o_ref[...]
    bb = bb_ref[...]
    mean = _seg_sum(o, bb) * (1.0 / A_HEAD)
    dlt = o - mean
    var = _seg_sum(dlt * dlt, bb) * (1.0 / A_HEAD)
    yn = dlt * lax.rsqrt(var + GN_EPS) * lnw_ref[...] + lnb_ref[...]
    ya = (yn + bonus_ref[...]) * g_ref[...]
    y = _bdot(ya, wout_ref[:a_width, :]) + _bdot(yb_ref[...], wout_ref[a_width:, :])
    y_ref[...] = x_ref[...] + y


def _ab_out(o, bonus, g, yb, x, prm, tm):
    n, d = x.shape
    row = lambda c: pl.BlockSpec((tm, c), lambda i: (i, 0))
    params = [prm["lnw"], prm["lnb"], prm["wout"], prm["bb"]]
    return pl.pallas_call(
        _ab_out_kernel, grid=(n // tm,),
        in_specs=[row(o.shape[1]), row(bonus.shape[1]), row(g.shape[1]), row(yb.shape[1]), row(d)]
        + [_const_spec(p.shape) for p in params],
        out_specs=row(d), out_shape=jax.ShapeDtypeStruct((n, d), F32),
        compiler_params=_cparams(("arbitrary",)),
    )(o, bonus, g, yb, x, *params)


def _mem_kv_kernel(x_ref, gm_ref, wkv_ref, kg_ref, k_ref, v_ref):
    d = x_ref.shape[1]
    hd = d // X_HEADS
    h = _rms_rows(x_ref[...], gm_ref[...])
    kv = _bdot(h, wkv_ref[...])
    kg = kg_ref[...]
    for hh in range(X_HEADS):
        kh = kv[:, hh * hd:(hh + 1) * hd]
        k_ref[:, hh * hd:(hh + 1) * hd] = _rms_rows(kh, kg)
    v_ref[...] = kv[:, d:]


def _mem_kv(mem, gm, wkv, kg, tm):
    n, d = mem.shape
    row = pl.BlockSpec((tm, d), lambda i: (i, 0))
    return pl.pallas_call(
        _mem_kv_kernel, grid=(n // tm,),
        in_specs=[row, _const_spec(gm.shape), _const_spec(wkv.shape), _const_spec(kg.shape)],
        out_specs=[row, row], out_shape=[jax.ShapeDtypeStruct((n, d), F32)] * 2,
        compiler_params=_cparams(("arbitrary",)),
    )(mem, gm, wkv, kg)


def _cross_kernel(x_ref, mk_ref, mv_ref, gm_ref, wq_ref, qg_ref, wo_ref, y_ref, o_scr):
    d = x_ref.shape[1]
    hd = d // X_HEADS
    x = x_ref[...]
    q = _bdot(_rms_rows(x, gm_ref[...]), wq_ref[...])
    qg = qg_ref[...]
    scale = hd ** -0.5
    for hh in range(X_HEADS):
        sl = slice(hh * hd, (hh + 1) * hd)
        qh = _rms_rows(q[:, sl], qg)
        s = _bdot_nt(qh, mk_ref[0, :, sl]) * scale
        p = jnp.exp(s - jnp.max(s, axis=-1, keepdims=True))
        p = p / jnp.sum(p, axis=-1, keepdims=True)
        o_scr[:, sl] = _bdot(p, mv_ref[0, :, sl])
    y_ref[...] = x + _bdot(o_scr[...], wo_ref[...])


def _cross(x, seg_len, tm, mk, mv, gm, wq, qg, wo):
    n, d = x.shape
    tiles_per_seq = seg_len // tm
    row = pl.BlockSpec((tm, d), lambda i: (i, 0))
    mem = pl.BlockSpec((1, mk.shape[1], d), lambda i: (i // tiles_per_seq, 0, 0))
    params = [gm, wq, qg, wo]
    return pl.pallas_call(
        _cross_kernel, grid=(n // tm,),
        in_specs=[row, mem, mem] + [_const_spec(p.shape) for p in params],
        out_specs=row, out_shape=jax.ShapeDtypeStruct((n, d), F32),
        scratch_shapes=[pltpu.VMEM((tm, d), F32)],
        compiler_params=_cparams(("arbitrary",)),
    )(x, mk, mv, *params)


def _ffn_kernel(carry_mode, seg_len, tiles_per_seq, n_chunks, *refs):
    x_ref, gm_ref, wup_ref, cw_ref, wdn_ref = refs[:5]
    if carry_mode:
        init_ref, y_ref, gate_o, tail = refs[5:9]
    else:
        e0_ref, e1_ref, y_ref, gate_o = refs[5:9]
    tm = x_ref.shape[0]
    d_ff = cw_ref.shape[1]
    ck = d_ff // n_chunks
    rows = lax.broadcasted_iota(jnp.int32, (tm, 1), 0)

    if carry_mode:
        @pl.when(pl.program_id(0) % tiles_per_seq == 0)
        def _():
            tail[...] = init_ref[0]

    x = x_ref[...]
    h = _rms_rows(x, gm_ref[...]).astype(BF16)
    acc = x
    for c in range(n_chunks):
        sl = slice(c * ck, (c + 1) * ck)
        gate = jnp.dot(h, wup_ref[:, sl], preferred_element_type=F32)
        up = jnp.dot(h, wup_ref[:, d_ff + c * ck:d_ff + (c + 1) * ck], preferred_element_type=F32)
        if carry_mode:
            xm1, xm2 = _taps(gate, rows, seg_len, tail[:, sl], None, None)
        else:
            xm1, xm2 = _taps(gate, rows, seg_len, None, e0_ref[:, sl], e1_ref[:, sl])
        cv = cw_ref[0:1, sl] * xm2 + cw_ref[1:2, sl] * xm1 + cw_ref[2:3, sl] * gate
        act = cv * _sigmoid(cv) * up
        acc = acc + _bdot(act, wdn_ref[sl, :])
        if carry_mode:
            last = gate[tm - SUBLANES:, :]
            tail[:, sl] = last
            gate_o[0, :, sl] = last
        else:
            gate_o[:, sl] = gate
    y_ref[...] = acc


def _ffn(x, seg_len, tm, conv_prev, gm, wup, cw, wdn, n_chunks=2):
    n, d = x.shape
    d_ff = cw.shape[1]
    n_seq = n // seg_len
    carry_mode = seg_len >= tm
    tiles_per_seq = max(seg_len // tm, 1)
    row = lambda c: pl.BlockSpec((tm, c), lambda i: (i, 0))
    params = [gm, wup, cw, wdn]
    in_specs = [row(d)] + [_const_spec(p.shape) for p in params]
    if carry_mode:
        seq_blk = pl.BlockSpec((1, SUBLANES, d_ff), lambda i: (i // tiles_per_seq, 0, 0))
        extra = [_tail8(conv_prev)]
        in_specs += [seq_blk]
        out_specs = [row(d), seq_blk]
        out_shape = [jax.ShapeDtypeStruct((n, d), F32), jax.ShapeDtypeStruct((n_seq, SUBLANES, d_ff), F32)]
        scratch = [pltpu.VMEM((SUBLANES, d_ff), F32)]
    else:
        extra = [jnp.repeat(conv_prev[:, 0], seg_len, axis=0), jnp.repeat(conv_prev[:, 1], seg_len, axis=0)]
        in_specs += [row(d_ff), row(d_ff)]
        out_specs = [row(d), row(d_ff)]
        out_shape = [jax.ShapeDtypeStruct((n, d), F32), jax.ShapeDtypeStruct((n, d_ff), F32)]
        scratch = []
    y, gate_t = pl.pallas_call(
        functools.partial(_ffn_kernel, carry_mode, seg_len, tiles_per_seq, n_chunks),
        grid=(n // tm,), in_specs=in_specs, out_specs=out_specs, out_shape=out_shape, scratch_shapes=scratch,
        compiler_params=_cparams(("arbitrary",)),
    )(x, *params, *extra)
    if carry_mode:
        conv_new = gate_t[:, SUBLANES - (CONV_W - 1):]
    else:
        conv_new = gate_t.reshape(n_seq, seg_len, d_ff)[:, -(CONV_W - 1):]
    return y, conv_new


def _log_sigmoid(x):
    return jnp.minimum(x, 0.0) - jnp.log(1.0 + jnp.exp(-jnp.abs(x)))


def _fox_in_kernel(with_cum, tiles_per_seq, *refs):
    x_ref, gm_ref, wqg_ref, wkvt_ref, wfl_ref, fb_ref, qg_ref, kgc_ref, bb_ref = refs[:9]
    if with_cum:
        wflt_ref, fbt_ref, tri_ref = refs[9:12]
        q_o, kt_o, vt_o, og_o, lf_o, cum_o, run = refs[12:19]
    else:
        q_o, kt_o, vt_o, og_o, lf_o = refs[9:14]
    tm, d = x_ref.shape
    heads = lf_o.shape[1]
    h = _rms_rows(x_ref[...], gm_ref[...]).astype(BF16)
    proj = jnp.dot(h, wqg_ref[...], preferred_element_type=F32)
    q = proj[:, :d]
    q = q * lax.rsqrt(_seg_sum(q * q, bb_ref[...]) * (1.0 / C_HEAD) + RMS_EPS) * qg_ref[...]
    q_o[...] = (q * (C_HEAD ** -0.5 * LOG2E)).astype(q_o.dtype)
    og_o[...] = proj[:, d:]
    kvt = lax.dot_general(wkvt_ref[...], h, (((1,), (1,)), ((), ())), preferred_element_type=F32)
    k3 = kvt[:d].reshape(heads, C_HEAD, tm)
    ms = jnp.sum(k3 * k3, axis=1, keepdims=True) * (1.0 / C_HEAD)
    gain = jnp.broadcast_to(kgc_ref[...][:, :1], (C_HEAD, tm))
    kt_o[0] = (k3 * lax.rsqrt(ms + RMS_EPS) * gain[None]).reshape(d, tm)
    vt_o[0] = kvt[d:]
    fl = jnp.dot(h, wfl_ref[...], preferred_element_type=F32) + fb_ref[...]
    lf_o[...] = _log_sigmoid(fl)[:, :heads]

    if with_cum:
        @pl.when(pl.program_id(0) % tiles_per_seq == 0)
        def _():
            run[...] = jnp.zeros_like(run)

        flt = lax.dot_general(wflt_ref[...], h, (((1,), (1,)), ((), ())), preferred_element_type=F32)
        lft = _log_sigmoid(flt + fbt_ref[...])
        tri = tri_ref[...]
        cum = run[...][:, :1]
        for part in _split3(lft):
            cum = cum + jnp.dot(part, tri, preferred_element_type=F32)
        cum_o[0] = cum
        run[...] = jnp.broadcast_to(cum[:, cum.shape[1] - 1:], run.shape)


def _fox_in(x, seg_len, tm, prm, with_cum, q_dtype):
    n, d = x.shape
    n_seq = n // seg_len
    heads = d // C_HEAD
    tiles_per_seq = seg_len // tm
    row = lambda c: pl.BlockSpec((tm, c), lambda i: (i, 0))
    tmin = pl.BlockSpec((1, d, tm), lambda i: (i // tiles_per_seq, 0, i % tiles_per_seq))
    params = [prm["gm"], prm["wqg"], prm["wkvt"], prm["wfl"], prm["fb"], prm["qg"], prm["kgc"], prm["bb"]]
    out_shape = [jax.ShapeDtypeStruct((n, d), q_dtype)] + [jax.ShapeDtypeStruct((n_seq, d, seg_len), F32)] * 2 + \
                [jax.ShapeDtypeStruct((n, d), F32), jax.ShapeDtypeStruct((n, heads), F32)]
    out_specs = [row(d), tmin, tmin, row(d), row(heads)]
    scratch = []
    if with_cum:
        params += [prm["wflt"], prm["fbt"], prm["tri"]]
        out_shape += [jax.ShapeDtypeStruct((n_seq, heads, seg_len), F32)]
        out_specs += [pl.BlockSpec((1, heads, tm), lambda i: (i // tiles_per_seq, 0, i % tiles_per_seq))]
        scratch = [pltpu.VMEM((heads, LANES), F32)]
    return pl.pallas_call(
        functools.partial(_fox_in_kernel, with_cum, tiles_per_seq),
        grid=(n // tm,), in_specs=[row(d)] + [_const_spec(p.shape) for p in params],
        out_specs=out_specs, out_shape=out_shape, scratch_shapes=scratch,
        compiler_params=_cparams(("arbitrary",)),
    )(x, *params)


def _fox_attn_kernel(tk, q_ref, k_ref, v_ref, cum_ref, o_ref, acc, mrun, s_a, s_b):
    tq = q_ref.shape[0]
    pair = pl.program_id(1)
    qi = pl.program_id(2)
    lane = lax.broadcasted_iota(jnp.int32, (1, LANES), 1)
    first = lane < C_HEAD
    q = q_ref[...]
    zero = jnp.zeros_like(q)
    qs = (jnp.where(first, q, zero), jnp.where(first, zero, q))
    acc[...] = jnp.zeros_like(acc)
    mrun[...] = jnp.full_like(mrun, NEG)

    top = lax.broadcasted_iota(jnp.int32, (LANES, 1), 0) < C_HEAD

    def cols(j):
        return pl.ds(pl.multiple_of(j * tk, tk), tk)

    def qk(j, s_ref):
        kb = k_ref[0, :, cols(j)].astype(BF16)
        for hh in range(2):
            s_ref[hh] = jnp.dot(qs[hh], kb, preferred_element_type=F32)

    def softmax_pv(j, s_ref, masked):
        vb = v_ref[0, :, cols(j)].astype(BF16)
        one = jnp.ones_like(vb)
        vs = (jnp.where(top, vb, one), jnp.where(top, one, vb))
        for hh in range(2):
            bias = cum_ref[0, 2 * pair + hh, pl.ds(j, 1), :] * (-LOG2E)
            s = s_ref[hh] + bias
            if masked:
                rr = lax.broadcasted_iota(jnp.int32, (tq, tk), 0)
                cc = lax.broadcasted_iota(jnp.int32, (tq, tk), 1)
                s = jnp.where(cc <= rr, s, NEG)
            m_old = mrun[hh]
            m_new = jnp.maximum(m_old, jnp.max(s, axis=-1, keepdims=True))
            p = jnp.exp2(s - m_new[:, :1])
            acc[hh] = jnp.exp2(m_old - m_new) * acc[hh] + _bdot_nt(p, vs[hh])
            mrun[hh] = m_new

    pairs = qi // 2
    qk(0, s_a)

    def body(jj, carry):
        qk(2 * jj + 1, s_b)
        softmax_pv(2 * jj, s_a, False)
        qk(2 * jj + 2, s_a)
        softmax_pv(2 * jj + 1, s_b, False)
        return carry

    lax.fori_loop(0, pairs, body, 0)

    @pl.when(qi == 2 * pairs)
    def _():
        softmax_pv(qi, s_a, True)

    @pl.when(qi != 2 * pairs)
    def _():
        qk(qi, s_b)
        softmax_pv(qi - 1, s_a, False)
        softmax_pv(qi, s_b, True)

    a0 = acc[0]
    a1 = acc[1]
    o_ref[...] = jnp.where(first, a0 / pltpu.roll(a0, C_HEAD, 1), a1 / pltpu.roll(a1, C_HEAD, 1))


def _fox_attn(q, kt, vt, cum_t, n_seq, seg_len, tq):
    n, d = q.shape
    heads = d // C_HEAD
    nq = seg_len // tq
    cum4 = cum_t.reshape(n_seq, heads, nq, tq)
    qblk = pl.BlockSpec((tq, LANES), lambda b, p, i: (b * nq + i, p))
    kblk = pl.BlockSpec((1, LANES, seg_len), lambda b, p, i: (b, p, 0))
    cblk = pl.BlockSpec((1, heads, nq, tq), lambda b, p, i: (b, 0, 0, 0))
    return pl.pallas_call(
        functools.partial(_fox_attn_kernel, tq),
        grid=(n_seq, d // LANES, nq),
        in_specs=[qblk, kblk, kblk, cblk],
        out_specs=qblk, out_shape=jax.ShapeDtypeStruct((n, d), F32),
        scratch_shapes=[pltpu.VMEM((2, tq, LANES), F32), pltpu.VMEM((2, tq, LANES), F32),
                        pltpu.VMEM((2, tq, tq), F32), pltpu.VMEM((2, tq, tq), F32)],
        compiler_params=_cparams(("arbitrary", "arbitrary", "arbitrary")),
    )(q, kt, vt, cum4)


def _fox_out_kernel(o_ref, og_ref, x_ref, w_ref, y_ref):
    y_ref[...] = x_ref[...] + _bdot(o_ref[...] * _sigmoid(og_ref[...]), w_ref[...])


def _fox_out(o, og, x, w, tm):
    n, d = x.shape
    row = pl.BlockSpec((tm, d), lambda i: (i, 0))
    return pl.pallas_call(
        _fox_out_kernel, grid=(n // tm,), in_specs=[row, row, row, _const_spec(w.shape)],
        out_specs=row, out_shape=jax.ShapeDtypeStruct((n, d), F32),
        compiler_params=_cparams(("arbitrary",)),
    )(o, og, x, w)


def _paged_kernel(pps, n_new, *refs):
    pt_ref = refs[0]
    q_ref = refs[1]
    k_refs = refs[2:2 + pps]
    v_refs = refs[2 + pps:2 + 2 * pps]
    lf_refs = refs[2 + 2 * pps:2 + 3 * pps]
    kn_ref, vn_ref, lfn_ref, hsel_ref, tri_ref, o_ref = refs[2 + 3 * pps:8 + 3 * pps]
    qexp, acc, mrun, lrun, crun, kt_all, vt_all, lf_all = refs[8 + 3 * pps:]
    del pt_ref
    step = pl.program_id(1)
    last = pl.num_programs(1) - 1
    rows, d = qexp.shape
    heads = rows // n_new
    rhead = _idiv(lax.broadcasted_iota(jnp.int32, (rows, 1), 0), n_new)
    lhead = _idiv(lax.broadcasted_iota(jnp.int32, (1, d), 1), C_HEAD)

    @pl.when(step == 0)
    def _():
        qf = q_ref[...].astype(F32)
        qt = jnp.concatenate([qf] * heads, axis=0)
        qexp[...] = jnp.where(rhead == lhead, qt, 0.0).astype(BF16)
        acc[...] = jnp.zeros_like(acc)
        mrun[...] = jnp.full_like(mrun, NEG)
        lrun[...] = jnp.zeros_like(lrun)
        crun[...] = jnp.zeros_like(crun)

    def block(kt, vt, lft, npg, new_tokens):
        s = jnp.dot(qexp[...], kt, preferred_element_type=F32)
        parts = [jnp.dot(hsel_ref[...], part, preferred_element_type=F32).astype(BF16) for part in _split3(lft)]
        stacked = jnp.concatenate([part[:, i * PAGE:(i + 1) * PAGE] for part in parts for i in range(npg)], axis=0)
        loc = jnp.dot(stacked, tri_ref[...], preferred_element_type=F32)
        off = crun[...]
        cols = []
        for i in range(npg):
            cum = off
            for pi in range(len(parts)):
                cum = cum + loc[(pi * npg + i) * rows:(pi * npg + i + 1) * rows, :]
            cols.append(s[:, i * PAGE:(i + 1) * PAGE] - cum * LOG2E)
            off = jnp.broadcast_to(cum[:, PAGE - 1:], (rows, PAGE))
        crun[...] = off
        s = cols[0] if npg == 1 else jnp.concatenate(cols, axis=1)
        if new_tokens:
            qpos = _pos_in_seq(lax.broadcasted_iota(jnp.int32, (rows, PAGE), 0), n_new)
            kpos = lax.broadcasted_iota(jnp.int32, (rows, PAGE), 1)
            s = jnp.where(kpos <= qpos, s, NEG)
        m_old = mrun[...][:, :1]
        m_new = jnp.maximum(m_old, jnp.max(s, axis=-1, keepdims=True))
        alpha = jnp.exp2(m_old - m_new)
        p = jnp.exp2(s - m_new)
        lrun[...] = alpha * lrun[...] + jnp.sum(p, axis=-1, keepdims=True)
        acc[...] = alpha * acc[...] + _bdot_nt(p, vt)
        mrun[...] = jnp.broadcast_to(m_new, mrun.shape)

    @pl.when(step < last)
    def _():
        for i in range(pps):
            kt_all[:, i * PAGE:(i + 1) * PAGE] = k_refs[i][0, 0].astype(BF16)
            vt_all[:, i * PAGE:(i + 1) * PAGE] = v_refs[i][0, 0].astype(BF16)
            lf_all[:, i * PAGE:(i + 1) * PAGE] = lf_refs[i][0, 0]
        block(kt_all[...], vt_all[...], lf_all[...], pps, False)

    @pl.when(step == last)
    def _():
        block(kn_ref[0].astype(BF16), vn_ref[0].astype(BF16), lfn_ref[0], 1, True)
        res = acc[...] / lrun[...][:, :1]
        out = jnp.zeros((n_new, d), F32)
        for hh in range(heads):
            out = out + jnp.where(lhead == hh, res[hh * n_new:(hh + 1) * n_new, :], 0.0)
        o_ref[...] = out


def _paged_attn(q, kt_new, vt_new, lf_new, cache_k, cache_v, cache_lf, layer, page_table, n_seq, n_new, pps):
    n, d = q.shape
    heads = d // C_HEAD
    rows = heads * n_new
    n_pages = page_table.shape[1]
    steps = n_pages // pps
    nc, n_pool = cache_k.shape[:2]
    ck = jnp.transpose(cache_k, (0, 1, 3, 4, 2)).reshape(nc, n_pool, d, PAGE)
    cv = jnp.transpose(cache_v, (0, 1, 3, 4, 2)).reshape(nc, n_pool, d, PAGE)
    cache_lf = jnp.transpose(cache_lf, (0, 1, 3, 2))
    pad = lambda at: jnp.pad(jnp.transpose(at.reshape(at.shape[0], n_seq, n_new), (1, 0, 2)),
                             ((0, 0), (0, 0), (0, PAGE - n_new)))
    hsel = (jnp.arange(rows)[:, None] // n_new == jnp.arange(heads)[None, :]).astype(BF16)
    tri = (jnp.arange(PAGE)[:, None] <= jnp.arange(PAGE)[None, :]).astype(BF16)

    def page_spec(i, width):
        return pl.BlockSpec((1, 1, width, PAGE),
                            lambda b, s, pt: (layer, pt[b, jnp.minimum(s, steps - 1) * pps + i], 0, 0))

    new_spec = lambda width: pl.BlockSpec((1, width, PAGE), lambda b, s, pt: (b, 0, 0))
    const = lambda shape: pl.BlockSpec(shape, lambda b, s, pt: (0, 0))
    qspec = pl.BlockSpec((n_new, d), lambda b, s, pt: (b, 0))
    grid_spec = pltpu.PrefetchScalarGridSpec(
        num_scalar_prefetch=1, grid=(n_seq, steps + 1),
        in_specs=[qspec] + [page_spec(i, d) for i in range(pps)] * 2 + [page_spec(i, heads) for i in range(pps)]
        + [new_spec(d), new_spec(d), new_spec(heads), const(hsel.shape), const(tri.shape)],
        out_specs=qspec,
        scratch_shapes=[pltpu.VMEM((rows, d), BF16), pltpu.VMEM((rows, d), F32), pltpu.VMEM((rows, PAGE), F32),
                        pltpu.VMEM((rows, PAGE), F32), pltpu.VMEM((rows, PAGE), F32),
                        pltpu.VMEM((d, pps * PAGE), BF16), pltpu.VMEM((d, pps * PAGE), BF16),
                        pltpu.VMEM((heads, pps * PAGE), F32)])
    return pl.pallas_call(
        functools.partial(_paged_kernel, pps, n_new), grid_spec=grid_spec,
        out_shape=jax.ShapeDtypeStruct((n, d), F32),
        compiler_params=_cparams(("arbitrary", "arbitrary")),
    )(page_table, q, *([ck] * pps), *([cv] * pps), *([cache_lf] * pps), pad(kt_new), pad(vt_new), pad(lf_new.T),
      hsel, tri)


def _block_ones(width, block):
    i = jnp.arange(width)
    return (i[:, None] // block == i[None, :] // block).astype(BF16)


def kernel(x_prompt, x_sample, mem_prompt, state_wkv, state_shift, state_conv, state_ffn_conv, cache_k, cache_v, cache_logf, cache_mem_k, cache_mem_v, page_table, norm_mix, norm_cross, norm_mem, norm_ffn, ab_in, ab_out, rw_mu, rw_w0, rw_w2, rw_a0, rw_a2, rw_g2, rw_kk, rw_ka, rw_rk, rw_lnw, rw_lnb, sc_conv_w, fox_in, fox_out, fox_qg, fox_kg, fox_fb, xq_w, xkv_w, xo_w, xq_g, xk_g, ffn_up, ffn_conv_w, ffn_down):
    bp, tp, d = x_prompt.shape
    bs, ts, _ = x_sample.shape
    n_mem = mem_prompt.shape[1]
    depth = norm_mix.shape[0]
    heads_c = d // C_HEAD
    a_width = rw_w0.shape[1]
    b_width = sc_conv_w.shape[2]
    d_ff = ffn_conv_w.shape[2]
    np_, ns_ = bp * tp, bs * ts
    tm_p = min(512, tp)
    row2 = lambda a: a.reshape(1, -1).astype(F32)
    bb = _block_ones(HALF, A_HEAD)

    yp = x_prompt.reshape(np_, d)
    ys = x_sample.reshape(ns_, d)
    mem = mem_prompt.reshape(bp * n_mem, d)
    pw, psh, pcv, pfc, pk, pv, plf, pmk, pmv = [], [], [], [], [], [], [], [], []
    sw, ssh, scv, sfc, sk, sv, slf = [], [], [], [], [], [], []

    for layer in range(depth):
        i = layer // 2
        if layer % 2 == 0:
            zrow = jnp.zeros((LANES // 2, a_width), F32)
            prm = dict(
                gm=row2(norm_mix[layer]), win=ab_in[i].astype(BF16), mu=row2(rw_mu[i]), w0=row2(rw_w0[i]),
                w2p=jnp.concatenate([rw_w2[i], zrow], axis=0).astype(BF16), a0=row2(rw_a0[i]),
                a2p=jnp.concatenate([zrow, rw_a2[i]], axis=0).astype(BF16), g2=rw_g2[i].astype(BF16),
                kkw=row2(rw_kk[i]), ka=row2(rw_ka[i]), rk=row2(rw_rk[i]), cw=sc_conv_w[i].astype(F32), bb=bb,
                lnw=row2(rw_lnw[i]), lnb=row2(rw_lnb[i]), wout=ab_out[i].astype(BF16))
            a_proj = ab_in.shape[2] - 3 * b_width
            ops, bonus, g, yb, sh, cv = _ab_in(yp, tp, tm_p, jnp.zeros((bp, a_proj), F32),
                                               jnp.zeros((bp, CONV_W - 1, b_width), F32), prm)
            o, st = _wkv(ops, bp, tp, jnp.zeros((bp, a_width // A_HEAD, A_HEAD, A_HEAD), F32), bb,
                         nb=min(4, bp), tc=min(128, tp), unroll=8)
            yp = _ab_out(o, bonus, g, yb, yp, prm, tm_p)
            pw.append(st); psh.append(sh); pcv.append(cv)
            ops, bonus, g, yb, sh, cv = _ab_in(ys, ts, ns_, state_shift[i], state_conv[i], prm)
            o, st = _wkv(ops, bs, ts, state_wkv[i], bb, nb=min(4, bs), tc=ts, unroll=8)
            ys = _ab_out(o, bonus, g, yb, ys, prm, ns_)
            sw.append(st); ssh.append(sh); scv.append(cv)
        else:
            w = fox_in[i]
            wfl = w[:, 3 * d:3 * d + heads_c]
            prm = dict(
                gm=row2(norm_mix[layer]),
                wqg=jnp.concatenate([w[:, :d], w[:, 3 * d + heads_c:]], axis=1).astype(BF16),
                wkvt=w[:, d:3 * d].T.astype(BF16),
                wfl=jnp.pad(wfl, ((0, 0), (0, LANES - heads_c))).astype(BF16),
                fb=jnp.pad(row2(fox_fb[i]), ((0, 0), (0, LANES - heads_c))),
                qg=row2(jnp.tile(fox_qg[i], heads_c)),
                kgc=jnp.broadcast_to(fox_kg[i].astype(F32)[:, None], (C_HEAD, LANES)), bb=bb,
                wflt=wfl.T.astype(BF16), fbt=fox_fb[i].reshape(heads_c, 1).astype(F32),
                tri=(jnp.arange(tm_p)[:, None] <= jnp.arange(tm_p)[None, :]).astype(BF16))
            wo = fox_out[i].astype(BF16)
            tok_major = lambda zt, b, t: jnp.transpose(zt.reshape(b, heads_c, C_HEAD, t), (0, 3, 1, 2))
            q, kt, vt, og, lf, cum_t = _fox_in(yp, tp, tm_p, prm, True, BF16)
            o = _fox_attn(q, kt, vt, cum_t, bp, tp, tm_p)
            yp = _fox_out(o, og, yp, wo, tm_p)
            pk.append(tok_major(kt, bp, tp)); pv.append(tok_major(vt, bp, tp))
            plf.append(lf.reshape(bp, tp, heads_c))
            q, kt, vt, og, lf = _fox_in(ys, ns_, ns_, prm, False, F32)
            o = _paged_attn(q, kt[0], vt[0], lf, cache_k, cache_v, cache_logf, i, page_table, bs, ts, pps=16)
            ys = _fox_out(o, og, ys, wo, ns_)
            sk.append(tok_major(kt, 1, ns_).reshape(bs, ts, heads_c, C_HEAD))
            sv.append(tok_major(vt, 1, ns_).reshape(bs, ts, heads_c, C_HEAD))
            slf.append(lf.reshape(bs, ts, heads_c))

        wq, wo_x = xq_w[layer].astype(BF16), xo_w[layer].astype(BF16)
        mk, mv = _mem_kv(mem, row2(norm_mem[layer]), xkv_w[layer].astype(BF16), row2(xk_g[layer]), n_mem)
        pmk.append(mk.reshape(bp, n_mem, X_HEADS, d // X_HEADS)); pmv.append(mv.reshape(bp, n_mem, X_HEADS, d // X_HEADS))
        gx, qg = row2(norm_cross[layer]), row2(xq_g[layer])
        yp = _cross(yp, tp, tm_p, mk.reshape(bp, n_mem, d), mv.reshape(bp, n_mem, d), gx, wq, qg, wo_x)
        ys = _cross(ys, ts, ts, cache_mem_k[layer].reshape(bs, n_mem, d), cache_mem_v[layer].reshape(bs, n_mem, d),
                    gx, wq, qg, wo_x)

        gf, wup, cwf, wdn = row2(norm_ffn[layer]), ffn_up[layer].astype(BF16), ffn_conv_w[layer].astype(F32), \
            ffn_down[layer].astype(BF16)
        yp, cvp = _ffn(yp, tp, tm_p, jnp.zeros((bp, CONV_W - 1, d_ff), F32), gf, wup, cwf, wdn)
        ys, cvs = _ffn(ys, ts, ns_, state_ffn_conv[layer], gf, wup, cwf, wdn)
        pfc.append(cvp); sfc.append(cvs)

    st = jnp.stack
    return (yp.reshape(bp, tp, d), ys.reshape(bs, ts, d), st(pw), st(psh), st(pcv), st(pfc), st(pk), st(pv), st(plf),
            st(pmk), st(pmv), st(sw), st(ssh), st(scv), st(sfc), st(sk), st(sv), st(slf))
```
